```python
import math
import jax, jax.numpy as jnp
from jax import lax
import numpy as np

D_MODEL = 4096
BATCH = 4
SEQ = 4096
DEPTH = 4

CHUNK = 64
EPS = 1e-6
MLSTM_HEADS = 8
MLSTM_DV = D_MODEL // MLSTM_HEADS
MLSTM_DQK = MLSTM_DV // 2
MLSTM_WIDTH = MLSTM_HEADS * MLSTM_DV
MLSTM_QK = MLSTM_HEADS * MLSTM_DQK
SSD_HEADDIM = 64
SSD_WIDTH = D_MODEL
SSD_HEADS = SSD_WIDTH // SSD_HEADDIM
SSD_GROUPS = 8
SSD_STATE = 128
SSD_CONV_TAPS = 4
SSD_BC = SSD_GROUPS * SSD_STATE
SSD_CONV_DIM = SSD_WIDTH + 2 * SSD_BC
CONV_CHANNELS = D_MODEL
CONV_TAPS = 31
FOX_HEADS = 32
FOX_HEADDIM = D_MODEL // FOX_HEADS
FOX_WIDTH = FOX_HEADS * FOX_HEADDIM
Q_BLOCK = 128
EVEN_SIZES = (MLSTM_QK, MLSTM_QK, MLSTM_WIDTH, MLSTM_WIDTH, MLSTM_HEADS, MLSTM_HEADS,
              SSD_WIDTH, SSD_WIDTH, SSD_BC, SSD_BC, SSD_HEADS)
EVEN_IN = sum(EVEN_SIZES)
EVEN_MIX = MLSTM_WIDTH + SSD_WIDTH
ODD_SIZES = (2 * CONV_CHANNELS, CONV_CHANNELS, FOX_WIDTH, FOX_WIDTH, FOX_WIDTH, FOX_WIDTH, FOX_HEADS)
ODD_IN = sum(ODD_SIZES)
ODD_MIX = CONV_CHANNELS + FOX_WIDTH
N_EVEN = (DEPTH + 1) // 2
N_ODD = DEPTH // 2

kernel_name = 'hybrid_mlstm_ssd_conformer_fox_trunk'


def _split(a, sizes):
    return jnp.split(a, np.cumsum(sizes)[:-1].tolist(), axis=-1)


def _rmsnorm(x, g):
    xf = x.astype(jnp.float32)
    y = xf * lax.rsqrt(jnp.mean(xf * xf, axis=-1, keepdims=True) + EPS)
    return (y * g.astype(jnp.float32)).astype(x.dtype)


def _group_rmsnorm(x, g, groups):
    shp = x.shape
    xf = x.astype(jnp.float32).reshape(shp[:-1] + (groups, shp[-1] // groups))
    y = xf * lax.rsqrt(jnp.mean(xf * xf, axis=-1, keepdims=True) + EPS)
    return (y.reshape(shp) * g.astype(jnp.float32)).astype(x.dtype)


def _layernorm(x, g, b):
    xf = x.astype(jnp.float32)
    mu = jnp.mean(xf, axis=-1, keepdims=True)
    xc = xf - mu
    y = xc * lax.rsqrt(jnp.mean(xc * xc, axis=-1, keepdims=True) + EPS)
    return (y * g.astype(jnp.float32) + b.astype(jnp.float32)).astype(x.dtype)


def _causal_dwconv(u, w, b):
    taps, ch = w.shape
    y = lax.conv_general_dilated(u, w[:, None, :].astype(u.dtype), window_strides=(1,),
                                 padding=[(taps - 1, 0)], dimension_numbers=('NWC', 'WIO', 'NWC'),
                                 feature_group_count=ch)
    return y + b.astype(u.dtype)


def _to_chunks(a):
    a = a.reshape((a.shape[0], a.shape[1] // CHUNK, CHUNK) + a.shape[2:])
    return jnp.moveaxis(a, 1, 0)


def _from_chunks(a):
    a = jnp.moveaxis(a, 0, 1)
    return a.reshape((a.shape[0], a.shape[1] * a.shape[2]) + a.shape[3:])


def _mlstm(q, k, v, i_pre, f_pre):
    f32 = jnp.float32
    bsz = q.shape[0]
    q = q.astype(f32) * (MLSTM_DQK ** -0.5)
    k = k.astype(f32)
    v = v.astype(f32)
    log_i = i_pre.astype(f32)
    log_f = jax.nn.log_sigmoid(f_pre.astype(f32))
    tril = jnp.tril(jnp.ones((CHUNK, CHUNK), dtype=bool))

    def step(carry, inp):
        cmem, nvec, m = carry
        qc, kc, vc, lic, lfc = inp
        b = jnp.cumsum(lfc, axis=1)
        dmat = b[:, :, None, :] - b[:, None, :, :] + lic[:, None, :, :]
        dmat = jnp.where(tril[None, :, :, None], dmat, -jnp.inf)
        inter = b + m[:, None, :]
        m_t = jnp.maximum(inter, jnp.max(dmat, axis=2))
        w = jnp.exp(dmat - m_t[:, :, None, :])
        sc = jnp.einsum('bthd,bshd->btsh', qc, kc) * w
        g_inter = jnp.exp(inter - m_t)
        num = jnp.einsum('btsh,bshv->bthv', sc, vc) + g_inter[..., None] * jnp.einsum('bthd,bhdv->bthv', qc, cmem)
        den = jnp.sum(sc, axis=2) + g_inter * jnp.einsum('bthd,bhd->bth', qc, nvec)
        h = num / jnp.maximum(jnp.abs(den), jnp.exp(-m_t))[..., None]
        b_last = b[:, -1]
        gk = b_last[:, None, :] - b + lic
        m_new = jnp.maximum(b_last + m, jnp.max(gk, axis=1))
        decay = jnp.exp(b_last + m - m_new)
        kw = kc * jnp.exp(gk - m_new[:, None, :])[..., None]
        cmem = decay[..., None, None] * cmem + jnp.einsum('bshd,bshv->bhdv', kw, vc)
        nvec = decay[..., None] * nvec + jnp.sum(kw, axis=1)
        return (cmem, nvec, m_new), h

    init = (jnp.zeros((bsz, MLSTM_HEADS, MLSTM_DQK, MLSTM_DV), f32),
            jnp.zeros((bsz, MLSTM_HEADS, MLSTM_DQK), f32),
            jnp.zeros((bsz, MLSTM_HEADS), f32))
    _, h = lax.scan(step, init, tuple(_to_chunks(a) for a in (q, k, v, log_i, log_f)))
    return _from_chunks(h)


def _ssd(x, dt, a_neg, bmat, cmat, d_skip):
    f32 = jnp.float32
    bsz, s = x.shape[:2]
    rep = SSD_HEADS // SSD_GROUPS
    xg = x.astype(f32).reshape(bsz, s, SSD_GROUPS, rep, SSD_HEADDIM)
    dtg = dt.astype(f32).reshape(bsz, s, SSD_GROUPS, rep)
    ag = dtg * a_neg.astype(f32).reshape(SSD_GROUPS, rep)
    bmat = bmat.astype(f32)
    cmat = cmat.astype(f32)
    tril = jnp.tril(jnp.ones((CHUNK, CHUNK), dtype=bool))

    def step(state, inp):
        xc, dtc, ac, bc, cc = inp
        cum = jnp.cumsum(ac, axis=1)
        seg = cum[:, :, None] - cum[:, None, :]
        lmat = jnp.exp(jnp.where(tril[None, :, :, None, None], seg, -jnp.inf))
        cb = jnp.einsum('btgn,bsgn->btsg', cc, bc)
        wts = cb[..., None] * lmat * dtc[:, None]
        y = jnp.einsum('btsgr,bsgrp->btgrp', wts, xc)
        y = y + jnp.einsum('btgn,bgrpn->btgrp', cc, state) * jnp.exp(cum)[..., None]
        cum_last = cum[:, -1]
        xw = xc * (jnp.exp(cum_last[:, None] - cum) * dtc)[..., None]
        state = jnp.exp(cum_last)[..., None, None] * state + jnp.einsum('bsgn,bsgrp->bgrpn', bc, xw)
        return state, y

    init = jnp.zeros((bsz, SSD_GROUPS, rep, SSD_HEADDIM, SSD_STATE), f32)
    _, y = lax.scan(step, init, tuple(_to_chunks(a) for a in (xg, dtg, ag, bmat, cmat)))
    y = _from_chunks(y) + d_skip.astype(f32).reshape(SSD_GROUPS, rep)[:, :, None] * xg
    return y.reshape(bsz, s, SSD_WIDTH)


def _fox(q, k, v, f_pre):
    f32 = jnp.float32
    bsz, s = q.shape[:2]
    nb = s // Q_BLOCK
    c = jnp.cumsum(jax.nn.log_sigmoid(f_pre.astype(f32)), axis=1)
    c_keys = jnp.moveaxis(c, 1, 2)
    qb = jnp.moveaxis(q.reshape(bsz, nb, Q_BLOCK, FOX_HEADS, FOX_HEADDIM), 1, 0)
    cb = jnp.moveaxis(c.reshape(bsz, nb, Q_BLOCK, FOX_HEADS), 1, 0)
    k_pos = jnp.arange(s)
    scale = FOX_HEADDIM ** -0.5

    def block(args):
        qi, ci, bi = args
        logits = jnp.einsum('bqhd,bkhd->bhqk', qi, k).astype(f32) * scale
        logits = logits + jnp.moveaxis(ci, 1, 2)[..., None] - c_keys[:, :, None, :]
        q_pos = bi * Q_BLOCK + jnp.arange(Q_BLOCK)
        logits = jnp.where(k_pos[None, :] <= q_pos[:, None], logits, -jnp.inf)
        p = jax.nn.softmax(logits, axis=-1).astype(v.dtype)
        return jnp.einsum('bhqk,bkhd->bqhd', p, v)

    o = lax.map(block, (qb, cb, jnp.arange(nb)))
    return jnp.moveaxis(o, 0, 1).reshape(bsz, s, FOX_WIDTH)


def _even_layer(h, w_in, ig_b, fg_b, m_norm_g, conv_w, conv_b, dt_b, a_log, d_skip, s_norm_g, w_out):
    f32 = jnp.float32
    bsz, s, _ = h.shape
    u = h @ w_in
    q, k, v, m_gate, i_pre, f_pre, z, xs, bm, cm, dt = _split(u, EVEN_SIZES)
    hm = _mlstm(q.reshape(bsz, s, MLSTM_HEADS, MLSTM_DQK), k.reshape(bsz, s, MLSTM_HEADS, MLSTM_DQK),
                v.reshape(bsz, s, MLSTM_HEADS, MLSTM_DV), i_pre + ig_b, f_pre + fg_b)
    ya = _group_rmsnorm(hm.reshape(bsz, s, MLSTM_WIDTH).astype(h.dtype), m_norm_g, MLSTM_HEADS) * jax.nn.silu(m_gate)
    xbc = jax.nn.silu(_causal_dwconv(jnp.concatenate([xs, bm, cm], axis=-1), conv_w, conv_b))
    xs, bm, cm = _split(xbc, (SSD_WIDTH, SSD_BC, SSD_BC))
    dt = jax.nn.softplus((dt + dt_b).astype(f32))
    y = _ssd(xs.reshape(bsz, s, SSD_HEADS, SSD_HEADDIM), dt, -jnp.exp(a_log.astype(f32)),
             bm.reshape(bsz, s, SSD_GROUPS, SSD_STATE), cm.reshape(bsz, s, SSD_GROUPS, SSD_STATE), d_skip)
    yb = _group_rmsnorm((y * jax.nn.silu(z.astype(f32))).astype(h.dtype), s_norm_g, SSD_GROUPS)
    return jnp.concatenate([ya, yb], axis=-1) @ w_out


def _odd_layer(h, w_in, conv_w, conv_b, ln_g, ln_b, fg_b, w_out):
    bsz, s, _ = h.shape
    u = h @ w_in
    glu, c_gate, q, k, v, f_gate, f_pre = _split(u, ODD_SIZES)
    a, g = jnp.split(glu, 2, axis=-1)
    c = _causal_dwconv(a * jax.nn.sigmoid(g), conv_w, conv_b)
    yc = jax.nn.silu(_layernorm(c, ln_g, ln_b)) * jax.nn.silu(c_gate)
    shp = (bsz, s, FOX_HEADS, FOX_HEADDIM)
    o = _fox(q.reshape(shp), k.reshape(shp), v.reshape(shp), f_pre + fg_b)
    yd = o * jax.nn.silu(f_gate)
    return jnp.concatenate([yc, yd], axis=-1) @ w_out


def setup_inputs(seed: int = 0) -> dict:
    key = jax.random.key(seed)
    ks = jax.random.split(key, 22)
    f32 = jnp.float32
    ne, no = N_EVEN, N_ODD

    def nrm(k, shape, scale):
        return jax.random.normal(k, shape, f32) * scale

    dt0 = jnp.exp(jax.random.uniform(ks[8], (ne, SSD_HEADS), f32, math.log(1e-3), math.log(1e-1)))
    return {
        'x': nrm(ks[0], (BATCH, SEQ, D_MODEL), 1.0),
        'e_norm_g': 1.0 + nrm(ks[1], (ne, D_MODEL), 0.02),
        'e_w_in': nrm(ks[2], (ne, D_MODEL, EVEN_IN), D_MODEL ** -0.5),
        'e_ig_b': nrm(ks[3], (ne, MLSTM_HEADS), 0.1),
        'e_fg_b': jax.random.uniform(ks[4], (ne, MLSTM_HEADS), f32, 3.0, 6.0),
        'e_mlstm_norm_g': 1.0 + nrm(ks[5], (ne, MLSTM_WIDTH), 0.02),
        'e_conv_w': nrm(ks[6], (ne, SSD_CONV_TAPS, SSD_CONV_DIM), SSD_CONV_TAPS ** -0.5),
        'e_conv_b': nrm(ks[7], (ne, SSD_CONV_DIM), 0.01),
        'e_dt_b': dt0 + jnp.log(-jnp.expm1(-dt0)),
        'e_a_log': jnp.log(jax.random.uniform(ks[9], (ne, SSD_HEADS), f32, 1.0, 16.0)),
        'e_d_skip': 1.0 + nrm(ks[10], (ne, SSD_HEADS), 0.1),
        'e_ssd_norm_g': 1.0 + nrm(ks[11], (ne, SSD_WIDTH), 0.02),
        'e_w_out': nrm(ks[12], (ne, EVEN_MIX, D_MODEL), EVEN_MIX ** -0.5),
        'o_norm_g': 1.0 + nrm(ks[13], (no, D_MODEL), 0.02),
        'o_w_in': nrm(ks[14], (no, D_MODEL, ODD_IN), D_MODEL ** -0.5),
        'o_conv_w': nrm(ks[15], (no, CONV_TAPS, CONV_CHANNELS), CONV_TAPS ** -0.5),
        'o_conv_b': nrm(ks[16], (no, CONV_CHANNELS), 0.01),
        'o_ln_g': 1.0 + nrm(ks[17], (no, CONV_CHANNELS), 0.02),
        'o_ln_b': nrm(ks[18], (no, CONV_CHANNELS), 0.01),
        'o_fg_b': jax.random.uniform(ks[19], (no, FOX_HEADS), f32, 2.0, 5.0),
        'o_w_out': nrm(ks[20], (no, ODD_MIX, D_MODEL), ODD_MIX ** -0.5),
        'final_norm_g': 1.0 + nrm(ks[21], (D_MODEL,), 0.02),
    }


def reference(x, e_norm_g, e_w_in, e_ig_b, e_fg_b, e_mlstm_norm_g, e_conv_w, e_conv_b, e_dt_b, e_a_log,
              e_d_skip, e_ssd_norm_g, e_w_out, o_norm_g, o_w_in, o_conv_w, o_conv_b, o_ln_g, o_ln_b, o_fg_b,
              o_w_out, final_norm_g):
    for layer in range(DEPTH):
        j = layer // 2
        if layer % 2 == 0:
            x = x + _even_layer(_rmsnorm(x, e_norm_g[j]), e_w_in[j], e_ig_b[j], e_fg_b[j], e_mlstm_norm_g[j],
                                e_conv_w[j], e_conv_b[j], e_dt_b[j], e_a_log[j], e_d_skip[j], e_ssd_norm_g[j],
                                e_w_out[j])
        else:
            x = x + _odd_layer(_rmsnorm(x, o_norm_g[j]), o_w_in[j], o_conv_w[j], o_conv_b[j], o_ln_g[j],
                               o_ln_b[j], o_fg_b[j], o_w_out[j])
    return _rmsnorm(x, final_norm_g)
```

```python
import functools

import jax
import jax.numpy as jnp
from jax import lax
from jax.experimental import pallas as pl
from jax.experimental.pallas import tpu as pltpu

F32 = jnp.float32
BF16 = jnp.bfloat16

EPS = 1e-6
DEPTH = 4
MLSTM_HEADS = 8
MLSTM_DQK = 256
MLSTM_DV = 512
MLSTM_CHUNK = 256
SSD_HEADS = 64
SSD_HEADDIM = 64
SSD_GROUPS = 8
SSD_REP = SSD_HEADS // SSD_GROUPS
SSD_STATE = 128
SSD_TAPS = 4
SSD_CHUNK = 128
SSD_GW = SSD_REP * SSD_HEADDIM
CONV_TAPS = 31
CONV_HALO = 32
CONV_TM = 256
FOX_HEADS = 32
FOX_HEADDIM = 128
FOX_TQ = 512
FOX_TK = 512
VMEM_LIMIT = 56 * 1024 * 1024

NEG_INF = float("-inf")


def _cparams(sem):
    return pltpu.CompilerParams(dimension_semantics=sem, vmem_limit_bytes=VMEM_LIMIT)


def _sigmoid(x):
    return 1.0 / (1.0 + jnp.exp(-x))


def _silu(x):
    return x * _sigmoid(x)


def _softplus(x):
    return jnp.maximum(x, 0.0) + jnp.log1p(jnp.exp(-jnp.abs(x)))


def _log_sigmoid(x):
    return jnp.minimum(x, 0.0) - jnp.log1p(jnp.exp(-jnp.abs(x)))


def _dot(a, b):
    return jnp.dot(a, b, preferred_element_type=F32)


def _dot_nt(a, b):
    return lax.dot_general(a, b, (((1,), (1,)), ((), ())), preferred_element_type=F32)


def _dot_tn(a, b):
    return lax.dot_general(a, b, (((0,), (0,)), ((), ())), preferred_element_type=F32)


def _dot01(x, m01):
    hi = x.astype(BF16)
    r1 = x - hi.astype(F32)
    mid = r1.astype(BF16)
    lo = (r1 - mid.astype(F32)).astype(BF16)
    return _dot(hi, m01) + _dot(mid, m01) + _dot(lo, m01)


def _triu01(n):
    r = lax.broadcasted_iota(jnp.int32, (n, n), 0)
    c = lax.broadcasted_iota(jnp.int32, (n, n), 1)
    return jnp.where(r <= c, 1.0, 0.0).astype(BF16)


def _causal_mask(n):
    r = lax.broadcasted_iota(jnp.int32, (n, n), 0)
    c = lax.broadcasted_iota(jnp.int32, (n, n), 1)
    return c <= r


def _rmsnorm_kernel(x_ref, g_ref, o_ref):
    x = x_ref[...]
    y = x * lax.rsqrt(jnp.mean(x * x, axis=-1, keepdims=True) + EPS)
    o_ref[...] = (y * g_ref[...]).astype(o_ref.dtype)


def _rmsnorm(x, g, out_dtype, tm=256):
    m, d = x.shape
    return pl.pallas_call(
        _rmsnorm_kernel,
        grid=(m // tm,),
        in_specs=[pl.BlockSpec((tm, d), lambda i: (i, 0)), pl.BlockSpec((1, d), lambda i: (0, 0))],
        out_specs=pl.BlockSpec((tm, d), lambda i: (i, 0)),
        out_shape=jax.ShapeDtypeStruct((m, d), out_dtype),
        compiler_params=_cparams(("parallel",)),
        name="rmsnorm",
    )(x, g.reshape(1, d))


def _matmul_kernel(x_ref, w_ref, o_ref):
    o_ref[...] = _dot(x_ref[...], w_ref[...]).astype(o_ref.dtype)


def _matmul(x, w, out_dtype, tm=1024, tn=512):
    m, k = x.shape
    n = w.shape[1]
    tm, tn = min(tm, m), min(tn, n)
    return pl.pallas_call(
        _matmul_kernel,
        grid=(m // tm, n // tn),
        in_specs=[pl.BlockSpec((tm, k), lambda i, j: (i, 0)), pl.BlockSpec((k, tn), lambda i, j: (0, j))],
        out_specs=pl.BlockSpec((tm, tn), lambda i, j: (i, j)),
        out_shape=jax.ShapeDtypeStruct((m, n), out_dtype),
        compiler_params=_cparams(("parallel", "arbitrary")),
        name="in_proj",
    )(x, w)


def _out_proj_kernel(ya_ref, yb_ref, wa_ref, wb_ref, x_ref, o_ref):
    o_ref[...] = x_ref[...] + _dot(ya_ref[...], wa_ref[...]) + _dot(yb_ref[...], wb_ref[...])


def _out_proj(ya, yb, wa, wb, xres, tm=512, tn=512):
    m, k = ya.shape
    n = wa.shape[1]
    tm, tn = min(tm, m), min(tn, n)
    return pl.pallas_call(
        _out_proj_kernel,
        grid=(m // tm, n // tn),
        in_specs=[pl.BlockSpec((tm, k), lambda i, j: (i, 0)), pl.BlockSpec((tm, k), lambda i, j: (i, 0)),
                  pl.BlockSpec((k, tn), lambda i, j: (0, j)), pl.BlockSpec((k, tn), lambda i, j: (0, j)),
                  pl.BlockSpec((tm, tn), lambda i, j: (i, j))],
        out_specs=pl.BlockSpec((tm, tn), lambda i, j: (i, j)),
        out_shape=jax.ShapeDtypeStruct((m, n), F32),
        compiler_params=_cparams(("parallel", "arbitrary")),
        name="out_proj",
    )(ya, yb, wa, wb, xres)


def _mlstm_gates_kernel(i_ref, f_ref, ib_ref, fb_ref, a_ref, b_ref, *, chunk):
    s = i_ref.shape[-1]
    li = i_ref[...] + ib_ref[...]
    lf = _log_sigmoid(f_ref[...] + fb_ref[...])
    tri = _triu01(chunk)
    for c in range(s // chunk):
        sl = slice(c * chunk, (c + 1) * chunk)
        b = _dot01(lf[:, sl], tri)
        b_ref[:, sl] = b
        a_ref[:, sl] = li[:, sl] - b


def _mlstm_gates(i_t, f_t, ig_b, fg_b, chunk):
    bsz, h, s = i_t.shape
    blk = pl.BlockSpec((None, h, s), lambda b: (b, 0, 0))
    bias = pl.BlockSpec((h, 1), lambda b: (0, 0))
    return pl.pallas_call(
        functools.partial(_mlstm_gates_kernel, chunk=chunk),
        grid=(bsz,),
        in_specs=[blk, blk, bias, bias],
        out_specs=[blk, blk],
        out_shape=[jax.ShapeDtypeStruct((bsz, h, s), F32)] * 2,
        compiler_params=_cparams(("parallel",)),
        name="mlstm_gates",
    )(i_t, f_t, ig_b.reshape(h, 1), fg_b.reshape(h, 1))


def _ssd_gates_kernel(dt_ref, dtb_ref, alog_ref, dto_ref, cum_ref, *, chunk):
    s = dt_ref.shape[-1]
    dt = _softplus(dt_ref[...] + dtb_ref[...])
    a = dt * (-jnp.exp(alog_ref[...]))
    dto_ref[...] = dt
    tri = _triu01(chunk)
    for c in range(s // chunk):
        sl = slice(c * chunk, (c + 1) * chunk)
        cum_ref[:, sl] = _dot01(a[:, sl], tri)


def _ssd_gates(dt_t, dt_b, a_log, chunk):
    bsz, h, s = dt_t.shape
    blk = pl.BlockSpec((None, h, s), lambda b: (b, 0, 0))
    par = pl.BlockSpec((h, 1), lambda b: (0, 0))
    return pl.pallas_call(
        functools.partial(_ssd_gates_kernel, chunk=chunk),
        grid=(bsz,),
        in_specs=[blk, par, par],
        out_specs=[blk, blk],
        out_shape=[jax.ShapeDtypeStruct((bsz, h, s), F32)] * 2,
        compiler_params=_cparams(("parallel",)),
        name="ssd_gates",
    )(dt_t, dt_b.reshape(h, 1), a_log.reshape(h, 1))


def _fox_gates_kernel(f_ref, fb_ref, c_ref, *, chunk):
    s = f_ref.shape[-1]
    lf = _log_sigmoid(f_ref[...] + fb_ref[...])
    tri = _triu01(chunk)
    carry = jnp.zeros((f_ref.shape[0], 1), F32)
    for c in range(s // chunk):
        sl = slice(c * chunk, (c + 1) * chunk)
        cc = _dot01(lf[:, sl], tri) + carry
        c_ref[:, sl] = cc
        carry = cc[:, chunk - 1:chunk]


def _fox_gates(f_t, fg_b, chunk=256):
    bsz, h, s = f_t.shape
    chunk = min(chunk, s)
    blk = pl.BlockSpec((None, h, s), lambda b: (b, 0, 0))
    return pl.pallas_call(
        functools.partial(_fox_gates_kernel, chunk=chunk),
        grid=(bsz,),
        in_specs=[blk, pl.BlockSpec((h, 1), lambda b: (0, 0))],
        out_specs=blk,
        out_shape=jax.ShapeDtypeStruct((bsz, h, s), F32),
        compiler_params=_cparams(("parallel",)),
        name="fox_gates",
    )(f_t, fg_b.reshape(h, 1))


def _mlstm_kernel(q_ref, k_ref, v_ref, mg_ref, arow_ref, acol_ref, bcol_ref, gain_ref, o_ref, c_scr, m_scr):
    L, dqk = q_ref.shape
    dv = v_ref.shape[-1]
    scale = dqk ** -0.5

    @pl.when(pl.program_id(2) == 0)
    def _():
        c_scr[...] = jnp.zeros_like(c_scr)
        m_scr[...] = jnp.zeros_like(m_scr)

    q = q_ref[...]
    k = k_ref[...]
    lane = lax.broadcasted_iota(jnp.int32, (L, 128), 1)
    vext = jnp.concatenate([v_ref[...], jnp.where(lane == 0, 1.0, 0.0).astype(BF16)], axis=1)
    a_row = arow_ref[...]
    a_col = acol_ref[...]
    b_col = bcol_ref[...]
    m_prev = m_scr[0:1, 0:1]

    dmat = jnp.where(_causal_mask(L), b_col + a_row, NEG_INF)
    inter = b_col + m_prev
    m_t = jnp.maximum(inter, jnp.max(dmat, axis=1, keepdims=True))
    w = jnp.exp(dmat - m_t)
    sc = (_dot_nt(q, k) * scale * w).astype(BF16)
    g_inter = jnp.exp(inter - m_t) * scale
    numden = _dot(sc, vext) + g_inter * _dot(q, c_scr[...].astype(BF16))
    num = numden[:, :dv]
    den = numden[:, dv:dv + 1]
    h = num / jnp.maximum(jnp.abs(den), jnp.exp(-m_t))

    hn = h * lax.rsqrt(jnp.mean(h * h, axis=-1, keepdims=True) + EPS) * gain_ref[...]
    o_ref[...] = (hn * _silu(mg_ref[...].astype(F32))).astype(o_ref.dtype)

    b_last = b_col[L - 1:L, :]
    gk = b_last + a_col
    m_new = jnp.maximum(b_last + m_prev, jnp.max(gk, axis=0, keepdims=True))
    decay = jnp.exp(b_last + m_prev - m_new)
    kw = (k.astype(F32) * jnp.exp(gk - m_new)).astype(BF16)
    c_scr[...] = decay * c_scr[...] + _dot_tn(kw, vext)
    m_scr[...] = jnp.broadcast_to(m_new, m_scr.shape)


def _mlstm(u3, cols, a, b, gain, chunk):
    bsz, s, _ = u3.shape
    h, dqk, dv = MLSTM_HEADS, MLSTM_DQK, MLSTM_DV
    q0, k0, v0, g0 = (cols["q"] // dqk, cols["k"] // dqk, cols["v"] // dv, cols["m_gate"] // dv)
    a_row = a.reshape(bsz, h, 1, s)
    a_col = a.reshape(bsz, h, s, 1)
    b_col = b.reshape(bsz, h, s, 1)
    col_spec = pl.BlockSpec((None, None, chunk, 1), lambda bi, hi, ci: (bi, hi, ci, 0))
    return pl.pallas_call(
        _mlstm_kernel,
        grid=(bsz, h, s // chunk),
        in_specs=[
            pl.BlockSpec((None, chunk, dqk), lambda bi, hi, ci: (bi, ci, q0 + hi)),
            pl.BlockSpec((None, chunk, dqk), lambda bi, hi, ci: (bi, ci, k0 + hi)),
            pl.BlockSpec((None, chunk, dv), lambda bi, hi, ci: (bi, ci, v0 + hi)),
            pl.BlockSpec((None, chunk, dv), lambda bi, hi, ci: (bi, ci, g0 + hi)),
            pl.BlockSpec((None, None, 1, chunk), lambda bi, hi, ci: (bi, hi, 0, ci)),
            col_spec, col_spec,
            pl.BlockSpec((1, dv), lambda bi, hi, ci: (0, hi)),
        ],
        out_specs=pl.BlockSpec((None, chunk, dv), lambda bi, hi, ci: (bi, ci, hi)),
        out_shape=jax.ShapeDtypeStruct((bsz, s, h * dv), BF16),
        scratch_shapes=[pltpu.VMEM((dqk, dv + 128), F32), pltpu.VMEM((8, 128), F32)],
        compiler_params=_cparams(("parallel", "parallel", "arbitrary")),
        name="mlstm",
    )(u3, u3, u3, u3, a_row, a_col, b_col, gain.reshape(1, h * dv))


def _ssd_kernel(xs_ref, bm_ref, cm_ref, z_ref, wx_ref, wb_ref, wc_ref, bx_ref, bb_ref, bc_ref,
                dtc_ref, dtr_ref, cumc_ref, cumr_ref, dskip_ref, gain_ref, o_ref,
                st_scr, xbuf, bbuf, cbuf, y_scr, xw_scr):
    L = xs_ref.shape[0]
    P, R = SSD_HEADDIM, SSD_REP
    first = pl.program_id(2) == 0

    def conv(src_ref, buf, w_ref, b_ref):
        @pl.when(first)
        def _():
            buf[0:8, :] = jnp.zeros((8, buf.shape[1]), F32)

        @pl.when(jnp.logical_not(first))
        def _():
            buf[0:8, :] = buf[L:L + 8, :]

        buf[8:8 + L, :] = src_ref[...].astype(F32)
        acc = b_ref[...]
        for t in range(SSD_TAPS):
            off = 8 - (SSD_TAPS - 1) + t
            acc = acc + w_ref[t:t + 1, :] * buf[off:off + L, :]
        return _silu(acc)

    @pl.when(first)
    def _():
        st_scr[...] = jnp.zeros_like(st_scr)

    xc = conv(xs_ref, xbuf, wx_ref, bx_ref)
    bc = conv(bm_ref, bbuf, wb_ref, bb_ref).astype(BF16)
    cc = conv(cm_ref, cbuf, wc_ref, bc_ref).astype(BF16)

    cb = _dot_nt(cc, bc)
    cs = _dot(cc, st_scr[...].astype(BF16))
    mask = _causal_mask(L)
    for r in range(R):
        sl = slice(r * P, (r + 1) * P)
        cum_c = cumc_ref[r]
        cum_r = cumr_ref[r]
        lmat = jnp.exp(jnp.where(mask, cum_c - cum_r, NEG_INF))
        wts = (cb * lmat * dtr_ref[r]).astype(BF16)
        xr = xc[:, sl]
        y_scr[:, sl] = _dot(wts, xr.astype(BF16)) + cs[:, sl] * jnp.exp(cum_c)
        cum_last = cum_c[L - 1:L, :]
        xw_scr[:, sl] = xr * (jnp.exp(cum_last - cum_c) * dtc_ref[r])
        st_scr[:, sl] = st_scr[:, sl] * jnp.exp(cum_last)

    st_scr[...] = st_scr[...] + _dot_tn(bc, xw_scr[...].astype(BF16))

    y = y_scr[...] + dskip_ref[...] * xc
    y = y * _silu(z_ref[...].astype(F32))
    yn = y * lax.rsqrt(jnp.mean(y * y, axis=-1, keepdims=True) + EPS) * gain_ref[...]
    o_ref[...] = yn.astype(o_ref.dtype)


def _ssd(u3, cols, conv_w, conv_b, dt, cum, d_skip, gain, chunk):
    bsz, s, _ = u3.shape
    g, r, gw, n = SSD_GROUPS, SSD_REP, SSD_GW, SSD_STATE
    width = g * gw
    x0, z0 = cols["xs"] // gw, cols["z"] // gw
    b0, c0 = cols["bm"] // n, cols["cm"] // n
    dt_c = dt.reshape(bsz, g * r, s, 1)
    dt_r = dt.reshape(bsz, g * r, 1, s)
    cum_c = cum.reshape(bsz, g * r, s, 1)
    cum_r = cum.reshape(bsz, g * r, 1, s)
    conv_b2 = conv_b.reshape(1, -1)
    dskip = jnp.repeat(d_skip, SSD_HEADDIM).reshape(1, width)
    col_spec = pl.BlockSpec((None, r, chunk, 1), lambda bi, gi, ci: (bi, gi, ci, 0))
    row_spec = pl.BlockSpec((None, r, 1, chunk), lambda bi, gi, ci: (bi, gi, 0, ci))
    return pl.pallas_call(
        _ssd_kernel,
        grid=(bsz, g, s // chunk),
        in_specs=[
            pl.BlockSpec((None, chunk, gw), lambda bi, gi, ci: (bi, ci, x0 + gi)),
            pl.BlockSpec((None, chunk, n), lambda bi, gi, ci: (bi, ci, b0 + gi)),
            pl.BlockSpec((None, chunk, n), lambda bi, gi, ci: (bi, ci, c0 + gi)),
            pl.BlockSpec((None, chunk, gw), lambda bi, gi, ci: (bi, ci, z0 + gi)),
            pl.BlockSpec((SSD_TAPS, gw), lambda bi, gi, ci: (0, gi)),
            pl.BlockSpec((SSD_TAPS, n), lambda bi, gi, ci: (0, width // n + gi)),
            pl.BlockSpec((SSD_TAPS, n), lambda bi, gi, ci: (0, width // n + g + gi)),
            pl.BlockSpec((1, gw), lambda bi, gi, ci: (0, gi)),
            pl.BlockSpec((1, n), lambda bi, gi, ci: (0, width // n + gi)),
            pl.BlockSpec((1, n), lambda bi, gi, ci: (0, width // n + g + gi)),
            col_spec, row_spec, col_spec, row_spec,
            pl.BlockSpec((1, gw), lambda bi, gi, ci: (0, gi)),
            pl.BlockSpec((1, gw), lambda bi, gi, ci: (0, gi)),
        ],
        out_specs=pl.BlockSpec((None, chunk, gw), lambda bi, gi, ci: (bi, ci, gi)),
        out_shape=jax.ShapeDtypeStruct((bsz, s, width), BF16),
        scratch_shapes=[pltpu.VMEM((n, gw), F32),
                        pltpu.VMEM((chunk + 8, gw), F32), pltpu.VMEM((chunk + 8, n), F32),
                        pltpu.VMEM((chunk + 8, n), F32),
                        pltpu.VMEM((chunk, gw), F32), pltpu.VMEM((chunk, gw), F32)],
        compiler_params=_cparams(("parallel", "parallel", "arbitrary")),
        name="ssd",
    )(u3, u3, u3, u3, conv_w, conv_w, conv_w, conv_b2, conv_b2, conv_b2,
      dt_c, dt_r, cum_c, cum_r, dskip, gain.reshape(1, width))


def _conformer_kernel(a_ref, g_ref, cg_ref, w_ref, b_ref, lng_ref, lnb_ref, o_ref, buf, cv):
    tm, ch = a_ref.shape
    halo = CONV_HALO
    first = pl.program_id(1) == 0

    @pl.when(first)
    def _():
        buf[0:halo, :] = jnp.zeros((halo, ch), F32)

    @pl.when(jnp.logical_not(first))
    def _():
        buf[0:halo, :] = buf[tm:tm + halo, :]

    buf[halo:halo + tm, :] = a_ref[...].astype(F32) * _sigmoid(g_ref[...].astype(F32))

    rs, cw = 32, 512

    def strip(idx, carry):
        c0 = pl.multiple_of(idx * cw, cw)
        for r0 in range(0, tm, rs):
            acc = jnp.broadcast_to(b_ref[:, pl.ds(c0, cw)], (rs, cw))
            for t in range(CONV_TAPS):
                off = r0 + halo - (CONV_TAPS - 1) + t
                acc = acc + w_ref[t:t + 1, pl.ds(c0, cw)] * buf[off:off + rs, pl.ds(c0, cw)]
            cv[r0:r0 + rs, pl.ds(c0, cw)] = acc
        return carry

    lax.fori_loop(0, ch // cw, strip, 0)

    c = cv[...]
    mu = jnp.mean(c, axis=-1, keepdims=True)
    xc = c - mu
    y = xc * lax.rsqrt(jnp.mean(xc * xc, axis=-1, keepdims=True) + EPS)
    y = y * lng_ref[...] + lnb_ref[...]
    o_ref[...] = (_silu(y) * _silu(cg_ref[...].astype(F32))).astype(o_ref.dtype)


def _conformer(u3, cols, conv_w, conv_b, ln_g, ln_b, tm):
    bsz, s, _ = u3.shape
    ch = conv_w.shape[1]
    a0, g0, c0 = cols["glu_a"] // ch, cols["glu_g"] // ch, cols["c_gate"] // ch
    w_pad = jnp.concatenate([conv_w, jnp.zeros((CONV_HALO - CONV_TAPS, ch), conv_w.dtype)], axis=0)
    row = pl.BlockSpec((1, ch), lambda bi, i: (0, 0))
    return pl.pallas_call(
        _conformer_kernel,
        grid=(bsz, s // tm),
        in_specs=[
            pl.BlockSpec((None, tm, ch), lambda bi, i: (bi, i, a0)),
            pl.BlockSpec((None, tm, ch), lambda bi, i: (bi, i, g0)),
            pl.BlockSpec((None, tm, ch), lambda bi, i: (bi, i, c0)),
            pl.BlockSpec((CONV_HALO, ch), lambda bi, i: (0, 0)),
            row, row, row,
        ],
        out_specs=pl.BlockSpec((None, tm, ch), lambda bi, i: (bi, i, 0)),
        out_shape=jax.ShapeDtypeStruct((bsz, s, ch), BF16),
        scratch_shapes=[pltpu.VMEM((CONV_HALO + tm, ch), F32), pltpu.VMEM((tm, ch), F32)],
        compiler_params=_cparams(("parallel", "arbitrary")),
        name="conformer_conv",
    )(u3, u3, u3, w_pad, conv_b.reshape(1, ch), ln_g.reshape(1, ch), ln_b.reshape(1, ch))


def _fox_kernel(q_ref, k_ref, v_ref, c_ref, fg_ref, o_ref, *, tk):
    tq, d = q_ref.shape
    scale = d ** -0.5
    i = pl.program_id(2)
    q = q_ref[...]
    nsub = tq // tk

    def block(j, carry, masked):
        m, l, acc = carry
        k0 = pl.multiple_of(j * tk, tk)
        kb = k_ref[pl.ds(k0, tk), :]
        vb = v_ref[pl.ds(k0, tk), :]
        s = _dot_nt(q, kb) * scale - c_ref[:, pl.ds(k0, tk)]
        if masked:
            qpos = i * tq + lax.broadcasted_iota(jnp.int32, (tq, tk), 0)
            kpos = k0 + lax.broadcasted_iota(jnp.int32, (tq, tk), 1)
            s = jnp.where(kpos <= qpos, s, NEG_INF)
        m_new = jnp.maximum(m, jnp.max(s, axis=-1, keepdims=True))
        alpha = jnp.exp(m - m_new)
        p = jnp.exp(s - m_new)
        l = alpha * l + jnp.sum(p, axis=-1, keepdims=True)
        acc = alpha * acc + _dot(p.astype(BF16), vb)
        return m_new, l, acc

    carry = (jnp.full((tq, 1), NEG_INF, F32), jnp.zeros((tq, 1), F32), jnp.zeros((tq, d), F32))
    for jj in range(nsub):
        carry = block(i * nsub + jj, carry, True)
    carry = lax.fori_loop(0, i * nsub, lambda j, c: block(j, c, False), carry)
    m, l, acc = carry
    o = acc / l
    o_ref[...] = (o * _silu(fg_ref[...].astype(F32))).astype(o_ref.dtype)


def _fox(u3, cols, c, tq, tk):
    bsz, s, _ = u3.shape
    h, d = FOX_HEADS, FOX_HEADDIM
    tq, tk = min(tq, s), min(tk, s)
    q0, k0, v0, g0 = (cols["q"] // d, cols["k"] // d, cols["v"] // d, cols["f_gate"] // d)
    c_row = c.reshape(bsz, h, 1, s)
    return pl.pallas_call(
        functools.partial(_fox_kernel, tk=tk),
        grid=(bsz, h, s // tq),
        in_specs=[
            pl.BlockSpec((None, tq, d), lambda bi, hi, i: (bi, i, q0 + hi)),
            pl.BlockSpec((None, s, d), lambda bi, hi, i: (bi, 0, k0 + hi)),
            pl.BlockSpec((None, s, d), lambda bi, hi, i: (bi, 0, v0 + hi)),
            pl.BlockSpec((None, None, 1, s), lambda bi, hi, i: (bi, hi, 0, 0)),
            pl.BlockSpec((None, tq, d), lambda bi, hi, i: (bi, i, g0 + hi)),
        ],
        out_specs=pl.BlockSpec((None, tq, d), lambda bi, hi, i: (bi, i, hi)),
        out_shape=jax.ShapeDtypeStruct((bsz, s, h * d), BF16),
        compiler_params=_cparams(("parallel", "parallel", "arbitrary")),
        name="fox_attention",
    )(u3, u3, u3, c_row, u3)


def _pad_cols(w, n):
    return jnp.concatenate([w, jnp.zeros((w.shape[0], n - w.shape[1]), w.dtype)], axis=1)


def _even_layer(xr, bsz, norm_g, w_in, ig_b, fg_b, m_norm_g, conv_w, conv_b, dt_b, a_log, d_skip, s_norm_g, w_out):
    m, d = xr.shape
    s = m // bsz
    qk, mw, sw, bc = MLSTM_HEADS * MLSTM_DQK, MLSTM_HEADS * MLSTM_DV, SSD_HEADS * SSD_HEADDIM, SSD_GROUPS * SSD_STATE
    n_a = 2 * qk + 2 * mw
    n_b = n_a + 2 * MLSTM_HEADS
    n_c = n_b + 2 * sw + 2 * bc
    w_big = jnp.concatenate([w_in[:, :n_a], w_in[:, n_b:n_c]], axis=1).astype(BF16)
    w_small = _pad_cols(jnp.concatenate([w_in[:, n_a:n_b], w_in[:, n_c:]], axis=1), 128).astype(BF16)
    cols = {"q": 0, "k": qk, "v": 2 * qk, "m_gate": 2 * qk + mw, "z": n_a, "xs": n_a + sw,
            "bm": n_a + 2 * sw, "cm": n_a + 2 * sw + bc}

    hn = _rmsnorm(xr, norm_g, BF16)
    u3 = _matmul(hn, w_big, BF16).reshape(bsz, s, -1)
    us = _matmul(hn, w_small, F32).reshape(bsz, s, 128)
    hh = MLSTM_HEADS
    i_t = jnp.swapaxes(us[:, :, :hh], 1, 2)
    f_t = jnp.swapaxes(us[:, :, hh:2 * hh], 1, 2)
    dt_t = jnp.swapaxes(us[:, :, 2 * hh:2 * hh + SSD_HEADS], 1, 2)

    mchunk = min(MLSTM_CHUNK, s)
    a, b = _mlstm_gates(i_t, f_t, ig_b, fg_b, mchunk)
    ya = _mlstm(u3, cols, a, b, m_norm_g, mchunk)

    schunk = min(SSD_CHUNK, s)
    dt, cum = _ssd_gates(dt_t, dt_b, a_log, schunk)
    yb = _ssd(u3, cols, conv_w, conv_b, dt, cum, d_skip, s_norm_g, schunk)

    w_out = w_out.astype(BF16)
    return _out_proj(ya.reshape(m, mw), yb.reshape(m, sw), w_out[:mw], w_out[mw:], xr)


def _odd_layer(xr, bsz, norm_g, w_in, conv_w, conv_b, ln_g, ln_b, fg_b, w_out):
    m, d = xr.shape
    s = m // bsz
    ch = conv_w.shape[1]
    fw = FOX_HEADS * FOX_HEADDIM
    n_a = 3 * ch + 4 * fw
    w_big = w_in[:, :n_a].astype(BF16)
    w_small = _pad_cols(w_in[:, n_a:], 128).astype(BF16)
    cols = {"glu_a": 0, "glu_g": ch, "c_gate": 2 * ch, "q": 3 * ch, "k": 3 * ch + fw, "v": 3 * ch + 2 * fw,
            "f_gate": 3 * ch + 3 * fw}

    hn = _rmsnorm(xr, norm_g, BF16)
    u3 = _matmul(hn, w_big, BF16).reshape(bsz, s, -1)
    us = _matmul(hn, w_small, F32).reshape(bsz, s, 128)
    f_t = jnp.swapaxes(us[:, :, :FOX_HEADS], 1, 2)

    yc = _conformer(u3, cols, conv_w, conv_b, ln_g, ln_b, min(CONV_TM, s))
    c = _fox_gates(f_t, fg_b)
    yd = _fox(u3, cols, c, FOX_TQ, FOX_TK)

    w_out = w_out.astype(BF16)
    return _out_proj(yc.reshape(m, ch), yd.reshape(m, fw), w_out[:ch], w_out[ch:], xr)


def kernel(x, e_norm_g, e_w_in, e_ig_b, e_fg_b, e_mlstm_norm_g, e_conv_w, e_conv_b, e_dt_b, e_a_log, e_d_skip, e_ssd_norm_g, e_w_out, o_norm_g, o_w_in, o_conv_w, o_conv_b, o_ln_g, o_ln_b, o_fg_b, o_w_out, final_norm_g):
    bsz, s, d = x.shape
    xr = x.reshape(bsz * s, d)
    for layer in range(DEPTH):
        j = layer // 2
        if layer % 2 == 0:
            xr = _even_layer(xr, bsz, e_norm_g[j], e_w_in[j], e_ig_b[j], e_fg_b[j], e_mlstm_norm_g[j], e_conv_w[j],
                             e_conv_b[j], e_dt_b[j], e_a_log[j], e_d_skip[j], e_ssd_norm_g[j], e_w_out[j])
        else:
            xr = _odd_layer(xr, bsz, o_norm_g[j], o_w_in[j], o_conv_w[j], o_conv_b[j], o_ln_g[j], o_ln_b[j],
                            o_fg_b[j], o_w_out[j])
    return _rmsnorm(xr, final_norm_g, F32).reshape(bsz, s, d)
```

```python
import functools
import math

import jax
import jax.numpy as jnp
from jax import lax
from jax.experimental import pallas as pl
from jax.experimental.pallas import tpu as pltpu

F32 = jnp.float32
BF16 = jnp.bfloat16

EPS = 1e-6
DEPTH = 4
LANES = 128
SUBLANES = 8
MLSTM_HEADS = 8
MLSTM_DQK = 256
MLSTM_DV = 512
MLSTM_CHUNK = 256
SSD_HEADS = 64
SSD_HEADDIM = 64
SSD_GROUPS = 8
SSD_REP = SSD_HEADS // SSD_GROUPS
SSD_STATE = 128
SSD_TAPS = 4
SSD_CHUNK = 128
SSD_GW = SSD_REP * SSD_HEADDIM
CONV_TAPS = 31
CONV_HALO = 32
CONV_TM = 256
FOX_HEADS = 32
FOX_HEADDIM = 128
FOX_TQ = 512
FOX_TK = 512
VMEM_LIMIT = 56 * 1024 * 1024

NEG_INF = float("-inf")
LOG2E = math.log2(math.e)


def _cparams(sem):
    return pltpu.CompilerParams(dimension_semantics=sem, vmem_limit_bytes=VMEM_LIMIT)


def _sigmoid(x):
    return 0.5 * jnp.tanh(0.5 * x) + 0.5


def _silu(x):
    return x * _sigmoid(x)


def _softplus(x):
    return jnp.maximum(x, 0.0) + jnp.log1p(jnp.exp(-jnp.abs(x)))


def _log_sigmoid(x):
    return jnp.minimum(x, 0.0) - jnp.log1p(jnp.exp(-jnp.abs(x)))


def _dot(a, b):
    return jnp.dot(a, b, preferred_element_type=F32)


def _dot_nt(a, b):
    return lax.dot_general(a, b, (((1,), (1,)), ((), ())), preferred_element_type=F32)


def _dot_tn(a, b):
    return lax.dot_general(a, b, (((0,), (0,)), ((), ())), preferred_element_type=F32)


def _split3(x):
    hi = x.astype(BF16)
    r1 = x - hi.astype(F32)
    mid = r1.astype(BF16)
    lo = (r1 - mid.astype(F32)).astype(BF16)
    return hi, mid, lo


def _dot01(x, m01):
    hi, mid, lo = _split3(x)
    return _dot(hi, m01) + _dot(mid, m01) + _dot(lo, m01)


def _triu01(n):
    r = lax.broadcasted_iota(jnp.int32, (n, n), 0)
    c = lax.broadcasted_iota(jnp.int32, (n, n), 1)
    return jnp.where(r <= c, 1.0, 0.0).astype(BF16)


def _causal_mask(n):
    r = lax.broadcasted_iota(jnp.int32, (n, n), 0)
    c = lax.broadcasted_iota(jnp.int32, (n, n), 1)
    return c <= r


def _rmsnorm_kernel(x_ref, g_ref, o_ref):
    x = x_ref[...]
    y = x * lax.rsqrt(jnp.mean(x * x, axis=-1, keepdims=True) + EPS)
    o_ref[...] = (y * g_ref[...]).astype(o_ref.dtype)


def _rmsnorm(x, g, out_dtype, tm=256):
    m, d = x.shape
    return pl.pallas_call(
        _rmsnorm_kernel,
        grid=(m // tm,),
        in_specs=[pl.BlockSpec((tm, d), lambda i: (i, 0)), pl.BlockSpec((1, d), lambda i: (0, 0))],
        out_specs=pl.BlockSpec((tm, d), lambda i: (i, 0)),
        out_shape=jax.ShapeDtypeStruct((m, d), out_dtype),
        compiler_params=_cparams(("parallel",)),
        name="rmsnorm",
    )(x, g.reshape(1, d))


def _matmul_kernel(x_ref, w_ref, o_ref):
    o_ref[...] = _dot(x_ref[...], w_ref[...]).astype(o_ref.dtype)


def _matmul(x, w, layer, col0, n, out_dtype, tm=1024, tn=512):
    m, k = x.shape
    tm, tn = min(tm, m), min(tn, n)
    c0 = col0 // tn
    return pl.pallas_call(
        _matmul_kernel,
        grid=(m // tm, n // tn),
        in_specs=[pl.BlockSpec((tm, k), lambda i, j: (i, 0)),
                  pl.BlockSpec((None, k, tn), lambda i, j: (layer, 0, c0 + j))],
        out_specs=pl.BlockSpec((tm, tn), lambda i, j: (i, j)),
        out_shape=jax.ShapeDtypeStruct((m, n), out_dtype),
        compiler_params=_cparams(("parallel", "arbitrary")),
        name="in_proj",
    )(x, w)


def _out_proj_kernel(ya_ref, yb_ref, wa_ref, wb_ref, x_ref, o_ref):
    o_ref[...] = x_ref[...] + _dot(ya_ref[...], wa_ref[...]) + _dot(yb_ref[...], wb_ref[...])


def _out_proj(ya, yb, w, layer, xres, tm=512, tn=512):
    m, k = ya.shape
    n = w.shape[-1]
    tm, tn = min(tm, m), min(tn, n)
    return pl.pallas_call(
        _out_proj_kernel,
        grid=(m // tm, n // tn),
        in_specs=[pl.BlockSpec((tm, k), lambda i, j: (i, 0)), pl.BlockSpec((tm, k), lambda i, j: (i, 0)),
                  pl.BlockSpec((None, k, tn), lambda i, j: (layer, 0, j)),
                  pl.BlockSpec((None, k, tn), lambda i, j: (layer, 1, j)),
                  pl.BlockSpec((tm, tn), lambda i, j: (i, j))],
        out_specs=pl.BlockSpec((tm, tn), lambda i, j: (i, j)),
        out_shape=jax.ShapeDtypeStruct((m, n), F32),
        compiler_params=_cparams(("parallel", "arbitrary")),
        name="out_proj",
    )(ya, yb, w, w, xres)


def _mlstm_gates_kernel(i_ref, f_ref, ib_ref, fb_ref, a_ref, b_ref, *, chunk):
    s = i_ref.shape[-1]
    li = i_ref[...] + ib_ref[...]
    lf = _log_sigmoid(f_ref[...] + fb_ref[...])
    tri = _triu01(chunk)
    for c in range(s // chunk):
        sl = slice(c * chunk, (c + 1) * chunk)
        b = _dot01(lf[:, sl], tri)
        b_ref[:, sl] = b
        a_ref[:, sl] = li[:, sl] - b


def _mlstm_gates(i_t, f_t, ig_b, fg_b, chunk):
    bsz, h, s = i_t.shape
    blk = pl.BlockSpec((None, h, s), lambda b: (b, 0, 0))
    bias = pl.BlockSpec((h, 1), lambda b: (0, 0))
    return pl.pallas_call(
        functools.partial(_mlstm_gates_kernel, chunk=chunk),
        grid=(bsz,),
        in_specs=[blk, blk, bias, bias],
        out_specs=[blk, blk],
        out_shape=[jax.ShapeDtypeStruct((bsz, h, s), F32)] * 2,
        compiler_params=_cparams(("parallel",)),
        name="mlstm_gates",
    )(i_t, f_t, ig_b.reshape(h, 1), fg_b.reshape(h, 1))


def _ssd_gates_kernel(dt_ref, dtb_ref, alog_ref, dto_ref, cum_ref, *, chunk):
    s = dt_ref.shape[-1]
    dt = _softplus(dt_ref[...] + dtb_ref[...])
    a = dt * (-jnp.exp(alog_ref[...]))
    dto_ref[...] = dt
    tri = _triu01(chunk)
    for c in range(s // chunk):
        sl = slice(c * chunk, (c + 1) * chunk)
        cum_ref[:, sl] = _dot01(a[:, sl], tri)


def _ssd_gates(dt_t, dt_b, a_log, chunk):
    bsz, h, s = dt_t.shape
    blk = pl.BlockSpec((None, h, s), lambda b: (b, 0, 0))
    par = pl.BlockSpec((h, 1), lambda b: (0, 0))
    return pl.pallas_call(
        functools.partial(_ssd_gates_kernel, chunk=chunk),
        grid=(bsz,),
        in_specs=[blk, par, par],
        out_specs=[blk, blk],
        out_shape=[jax.ShapeDtypeStruct((bsz, h, s), F32)] * 2,
        compiler_params=_cparams(("parallel",)),
        name="ssd_gates",
    )(dt_t, dt_b.reshape(h, 1), a_log.reshape(h, 1))


def _fox_gates_kernel(f_ref, fb_ref, c_ref, *, chunk):
    s = f_ref.shape[-1]
    lf = _log_sigmoid(f_ref[...] + fb_ref[...])
    tri = _triu01(chunk)
    carry = jnp.zeros((f_ref.shape[0], 1), F32)
    for c in range(s // chunk):
        sl = slice(c * chunk, (c + 1) * chunk)
        cc = _dot01(lf[:, sl], tri) + carry
        c_ref[:, sl] = cc
        carry = cc[:, chunk - 1:chunk]


def _fox_gates(f_t, fg_b, chunk=256):
    bsz, h, s = f_t.shape
    chunk = min(chunk, s)
    blk = pl.BlockSpec((None, h, s), lambda b: (b, 0, 0))
    return pl.pallas_call(
        functools.partial(_fox_gates_kernel, chunk=chunk),
        grid=(bsz,),
        in_specs=[blk, pl.BlockSpec((h, 1), lambda b: (0, 0))],
        out_specs=blk,
        out_shape=jax.ShapeDtypeStruct((bsz, h, s), F32),
        compiler_params=_cparams(("parallel",)),
        name="fox_gates",
    )(f_t, fg_b.reshape(h, 1))


def _mlstm_kernel(q_ref, k_ref, v_ref, mg_ref, arow_ref, acol_ref, bcol_ref, gain_ref, o_ref, c_scr, m_scr):
    L, dqk = q_ref.shape
    dv = v_ref.shape[-1]
    scale = dqk ** -0.5

    @pl.when(pl.program_id(2) == 0)
    def _():
        c_scr[...] = jnp.zeros_like(c_scr)
        m_scr[...] = jnp.zeros_like(m_scr)

    q = q_ref[...]
    k = k_ref[...]
    lane = lax.broadcasted_iota(jnp.int32, (L, LANES), 1)
    vext = jnp.concatenate([v_ref[...], jnp.where(lane == 0, 1.0, 0.0).astype(BF16)], axis=1)
    a_row = arow_ref[...]
    a_col = acol_ref[...]
    b_col = bcol_ref[...]
    m_prev = m_scr[0:1, 0:1]

    dmat = jnp.where(_causal_mask(L), b_col + a_row, NEG_INF)
    inter = b_col + m_prev
    m_t = jnp.maximum(inter, jnp.max(dmat, axis=1, keepdims=True))
    w = jnp.exp(dmat - m_t)
    sc = (_dot_nt(q, k) * scale * w).astype(BF16)
    g_inter = jnp.exp(inter - m_t) * scale
    numden = _dot(sc, vext) + g_inter * _dot(q, c_scr[...].astype(BF16))
    num = numden[:, :dv]
    den = numden[:, dv:dv + 1]
    h = num / jnp.maximum(jnp.abs(den), jnp.exp(-m_t))

    hn = h * lax.rsqrt(jnp.mean(h * h, axis=-1, keepdims=True) + EPS) * gain_ref[...]
    o_ref[...] = (hn * _silu(mg_ref[...].astype(F32))).astype(o_ref.dtype)

    b_last = b_col[L - 1:L, :]
    gk = b_last + a_col
    m_new = jnp.maximum(b_last + m_prev, jnp.max(gk, axis=0, keepdims=True))
    decay = jnp.exp(b_last + m_prev - m_new)
    kw = (k.astype(F32) * jnp.exp(gk - m_new)).astype(BF16)
    c_scr[...] = decay * c_scr[...] + _dot_tn(kw, vext)
    m_scr[...] = jnp.broadcast_to(m_new, m_scr.shape)


def _mlstm(u3, cols, a, b, gain, chunk):
    bsz, s, _ = u3.shape
    h, dqk, dv = MLSTM_HEADS, MLSTM_DQK, MLSTM_DV
    q0, k0, v0, g0 = (cols["q"] // dqk, cols["k"] // dqk, cols["v"] // dv, cols["m_gate"] // dv)
    a_row = a.reshape(bsz, h, 1, s)
    a_col = a.reshape(bsz, h, s, 1)
    b_col = b.reshape(bsz, h, s, 1)
    col_spec = pl.BlockSpec((None, None, chunk, 1), lambda bi, hi, ci: (bi, hi, ci, 0))
    return pl.pallas_call(
        _mlstm_kernel,
        grid=(bsz, h, s // chunk),
        in_specs=[
            pl.BlockSpec((None, chunk, dqk), lambda bi, hi, ci: (bi, ci, q0 + hi)),
            pl.BlockSpec((None, chunk, dqk), lambda bi, hi, ci: (bi, ci, k0 + hi)),
            pl.BlockSpec((None, chunk, dv), lambda bi, hi, ci: (bi, ci, v0 + hi)),
            pl.BlockSpec((None, chunk, dv), lambda bi, hi, ci: (bi, ci, g0 + hi)),
            pl.BlockSpec((None, None, 1, chunk), lambda bi, hi, ci: (bi, hi, 0, ci)),
            col_spec, col_spec,
            pl.BlockSpec((1, dv), lambda bi, hi, ci: (0, hi)),
        ],
        out_specs=pl.BlockSpec((None, chunk, dv), lambda bi, hi, ci: (bi, ci, hi)),
        out_shape=jax.ShapeDtypeStruct((bsz, s, h * dv), BF16),
        scratch_shapes=[pltpu.VMEM((dqk, dv + LANES), F32), pltpu.VMEM((SUBLANES, LANES), F32)],
        compiler_params=_cparams(("parallel", "parallel", "arbitrary")),
        name="mlstm",
    )(u3, u3, u3, u3, a_row, a_col, b_col, gain.reshape(1, h * dv))


def _ssd_kernel(xs_ref, bm_ref, cm_ref, z_ref, wx_ref, wb_ref, wc_ref, bx_ref, bb_ref, bc_ref,
                dtc_ref, dtr_ref, cumc_ref, cumr_ref, dskip_ref, gain_ref, o_ref,
                st_scr, xbuf, bbuf, cbuf):
    L = xs_ref.shape[0]
    P, R = SSD_HEADDIM, SSD_REP
    first = pl.program_id(2) == 0

    def conv(src_ref, buf, w_ref, b_ref):
        @pl.when(first)
        def _():
            buf[0:SUBLANES, :] = jnp.zeros((SUBLANES, buf.shape[1]), F32)

        @pl.when(jnp.logical_not(first))
        def _():
            buf[0:SUBLANES, :] = buf[L:L + SUBLANES, :]

        buf[SUBLANES:SUBLANES + L, :] = src_ref[...].astype(F32)
        acc = b_ref[...]
        for t in range(SSD_TAPS):
            off = SUBLANES - (SSD_TAPS - 1) + t
            acc = acc + w_ref[t:t + 1, :] * buf[off:off + L, :]
        return _silu(acc)

    @pl.when(first)
    def _():
        st_scr[...] = jnp.zeros_like(st_scr)

    xc = conv(xs_ref, xbuf, wx_ref, bx_ref)
    bc = conv(bm_ref, bbuf, wb_ref, bb_ref).astype(BF16)
    cc = conv(cm_ref, cbuf, wc_ref, bc_ref).astype(BF16)

    cb = _dot_nt(cc, bc)
    st = st_scr[...]
    cs = _dot(cc, st.astype(BF16))
    mask = _causal_mask(L)
    low = lax.broadcasted_iota(jnp.int32, (1, 2 * P), 1) < P

    def head_weights(r):
        lmat = jnp.exp(jnp.where(mask, cumc_ref[r] - cumr_ref[r], NEG_INF))
        return (cb * lmat * dtr_ref[r]).astype(BF16)

    y_parts, xw_parts, dec_parts = [], [], []
    for pr in range(R // 2):
        r0, r1 = 2 * pr, 2 * pr + 1
        sl = slice(pr * 2 * P, (pr + 1) * 2 * P)
        xp = xc[:, sl]
        c0, c1 = cumc_ref[r0], cumc_ref[r1]
        wts = jnp.concatenate([head_weights(r0), head_weights(r1)], axis=1)
        xbd = jnp.concatenate([jnp.where(low, xp, 0.0), jnp.where(low, 0.0, xp)], axis=0)
        y_parts.append(_dot(wts, xbd.astype(BF16)) + cs[:, sl] * jnp.where(low, jnp.exp(c0), jnp.exp(c1)))
        l0, l1 = c0[L - 1:L, :], c1[L - 1:L, :]
        xw_parts.append(xp * jnp.where(low, jnp.exp(l0 - c0) * dtc_ref[r0], jnp.exp(l1 - c1) * dtc_ref[r1]))
        dec_parts.append(jnp.where(low, jnp.exp(l0), jnp.exp(l1)))

    xw = jnp.concatenate(xw_parts, axis=1).astype(BF16)
    st_scr[...] = st * jnp.concatenate(dec_parts, axis=1) + _dot_tn(bc, xw)

    y = jnp.concatenate(y_parts, axis=1) + dskip_ref[...] * xc
    y = y * _silu(z_ref[...].astype(F32))
    yn = y * lax.rsqrt(jnp.mean(y * y, axis=-1, keepdims=True) + EPS) * gain_ref[...]
    o_ref[...] = yn.astype(o_ref.dtype)


def _ssd(u3, cols, conv_w, conv_b, dt, cum, d_skip, gain, chunk):
    bsz, s, _ = u3.shape
    g, r, gw, n = SSD_GROUPS, SSD_REP, SSD_GW, SSD_STATE
    width = g * gw
    x0, z0 = cols["xs"] // gw, cols["z"] // gw
    b0, c0 = cols["bm"] // n, cols["cm"] // n
    dt_c = dt.reshape(bsz, g * r, s, 1)
    dt_r = dt.reshape(bsz, g * r, 1, s)
    cum_c = cum.reshape(bsz, g * r, s, 1)
    cum_r = cum.reshape(bsz, g * r, 1, s)
    conv_b2 = conv_b.reshape(1, -1)
    dskip = jnp.repeat(d_skip, SSD_HEADDIM).reshape(1, width)
    col_spec = pl.BlockSpec((None, r, chunk, 1), lambda bi, gi, ci: (bi, gi, ci, 0))
    row_spec = pl.BlockSpec((None, r, 1, chunk), lambda bi, gi, ci: (bi, gi, 0, ci))
    return pl.pallas_call(
        _ssd_kernel,
        grid=(bsz, g, s // chunk),
        in_specs=[
            pl.BlockSpec((None, chunk, gw), lambda bi, gi, ci: (bi, ci, x0 + gi)),
            pl.BlockSpec((None, chunk, n), lambda bi, gi, ci: (bi, ci, b0 + gi)),
            pl.BlockSpec((None, chunk, n), lambda bi, gi, ci: (bi, ci, c0 + gi)),
            pl.BlockSpec((None, chunk, gw), lambda bi, gi, ci: (bi, ci, z0 + gi)),
            pl.BlockSpec((SSD_TAPS, gw), lambda bi, gi, ci: (0, gi)),
            pl.BlockSpec((SSD_TAPS, n), lambda bi, gi, ci: (0, width // n + gi)),
            pl.BlockSpec((SSD_TAPS, n), lambda bi, gi, ci: (0, width // n + g + gi)),
            pl.BlockSpec((1, gw), lambda bi, gi, ci: (0, gi)),
            pl.BlockSpec((1, n), lambda bi, gi, ci: (0, width // n + gi)),
            pl.BlockSpec((1, n), lambda bi, gi, ci: (0, width // n + g + gi)),
            col_spec, row_spec, col_spec, row_spec,
            pl.BlockSpec((1, gw), lambda bi, gi, ci: (0, gi)),
            pl.BlockSpec((1, gw), lambda bi, gi, ci: (0, gi)),
        ],
        out_specs=pl.BlockSpec((None, chunk, gw), lambda bi, gi, ci: (bi, ci, gi)),
        out_shape=jax.ShapeDtypeStruct((bsz, s, width), BF16),
        scratch_shapes=[pltpu.VMEM((n, gw), F32),
                        pltpu.VMEM((chunk + SUBLANES, gw), F32), pltpu.VMEM((chunk + SUBLANES, n), F32),
                        pltpu.VMEM((chunk + SUBLANES, n), F32)],
        compiler_params=_cparams(("parallel", "parallel", "arbitrary")),
        name="ssd",
    )(u3, u3, u3, u3, conv_w, conv_w, conv_w, conv_b2, conv_b2, conv_b2,
      dt_c, dt_r, cum_c, cum_r, dskip, gain.reshape(1, width))


def _conformer_kernel(a_ref, g_ref, cg_ref, w_ref, b_ref, lng_ref, lnb_ref, o_ref, buf, cv):
    tm, ch = a_ref.shape
    halo = CONV_HALO
    first = pl.program_id(1) == 0

    @pl.when(first)
    def _():
        buf[0:halo, :] = jnp.zeros((halo, ch), F32)

    @pl.when(jnp.logical_not(first))
    def _():
        buf[0:halo, :] = buf[tm:tm + halo, :]

    buf[halo:halo + tm, :] = a_ref[...].astype(F32) * _sigmoid(g_ref[...].astype(F32))

    rs, cw = 32, 512
    off0 = halo - (CONV_TAPS - 1)

    def strip(idx, carry):
        c0 = pl.multiple_of(idx * cw, cw)
        for r0 in range(0, tm, rs):
            acc = jnp.broadcast_to(b_ref[:, pl.ds(c0, cw)], (rs, cw))
            for p in range(SUBLANES):
                ext = rs if p == 0 else rs + SUBLANES
                tp = None
                for t in range(CONV_TAPS):
                    if (off0 + t) % SUBLANES != p:
                        continue
                    base = r0 + off0 + t - p
                    term = w_ref[t:t + 1, pl.ds(c0, cw)] * buf[base:base + ext, pl.ds(c0, cw)]
                    tp = term if tp is None else tp + term
                acc = acc + (tp if p == 0 else tp[p:p + rs])
            cv[r0:r0 + rs, pl.ds(c0, cw)] = acc
        return carry

    lax.fori_loop(0, ch // cw, strip, 0)

    rb = 2 * SUBLANES

    def norm_rows(idx, carry):
        r0 = pl.multiple_of(idx * rb, rb)
        c = cv[pl.ds(r0, rb), :]
        mu = jnp.mean(c, axis=-1, keepdims=True)
        xc = c - mu
        y = xc * lax.rsqrt(jnp.mean(xc * xc, axis=-1, keepdims=True) + EPS)
        y = y * lng_ref[...] + lnb_ref[...]
        gate = _silu(cg_ref[pl.ds(r0, rb), :].astype(F32))
        o_ref[pl.ds(r0, rb), :] = (_silu(y) * gate).astype(o_ref.dtype)
        return carry

    lax.fori_loop(0, tm // rb, norm_rows, 0)


def _conformer(u3, cols, conv_w, conv_b, ln_g, ln_b, tm):
    bsz, s, _ = u3.shape
    ch = conv_w.shape[1]
    a0, g0, c0 = cols["glu_a"] // ch, cols["glu_g"] // ch, cols["c_gate"] // ch
    w_pad = jnp.concatenate([conv_w, jnp.zeros((CONV_HALO - CONV_TAPS, ch), conv_w.dtype)], axis=0)
    row = pl.BlockSpec((1, ch), lambda bi, i: (0, 0))
    return pl.pallas_call(
        _conformer_kernel,
        grid=(bsz, s // tm),
        in_specs=[
            pl.BlockSpec((None, tm, ch), lambda bi, i: (bi, i, a0)),
            pl.BlockSpec((None, tm, ch), lambda bi, i: (bi, i, g0)),
            pl.BlockSpec((None, tm, ch), lambda bi, i: (bi, i, c0)),
            pl.BlockSpec((CONV_HALO, ch), lambda bi, i: (0, 0)),
            row, row, row,
        ],
        out_specs=pl.BlockSpec((None, tm, ch), lambda bi, i: (bi, i, 0)),
        out_shape=jax.ShapeDtypeStruct((bsz, s, ch), BF16),
        scratch_shapes=[pltpu.VMEM((CONV_HALO + tm, ch), F32), pltpu.VMEM((tm, ch), F32)],
        compiler_params=_cparams(("parallel", "arbitrary")),
        name="conformer_conv",
    )(u3, u3, u3, w_pad, conv_b.reshape(1, ch), ln_g.reshape(1, ch), ln_b.reshape(1, ch))


def _fox_kernel(q_ref, k_ref, v_ref, c_ref, fg_ref, o_ref, *, tk):
    tq, d = q_ref.shape
    i = pl.program_id(2)
    q2 = (q_ref[...].astype(F32) * (d ** -0.5 * LOG2E)).astype(BF16)

    def block(j, carry, masked):
        m, l, acc = carry
        k0 = pl.multiple_of(j * tk, tk)
        s = _dot_nt(q2, k_ref[pl.ds(k0, tk), :]) - c_ref[:, pl.ds(k0, tk)] * LOG2E
        if masked:
            qpos = i * tq + lax.broadcasted_iota(jnp.int32, (tq, tk), 0)
            kpos = k0 + lax.broadcasted_iota(jnp.int32, (tq, tk), 1)
            s = jnp.where(kpos <= qpos, s, NEG_INF)
        m_new = jnp.maximum(m, jnp.max(s, axis=-1, keepdims=True))
        alpha = jnp.exp2(m - m_new)
        p = jnp.exp2(s - m_new)
        l = alpha * l + jnp.sum(p, axis=-1, keepdims=True)
        acc = alpha * acc + _dot(p.astype(BF16), v_ref[pl.ds(k0, tk), :])
        return m_new, l, acc

    carry = (jnp.full((tq, 1), NEG_INF, F32), jnp.zeros((tq, 1), F32), jnp.zeros((tq, d), F32))
    carry = block(i, carry, True)
    _, l, acc = lax.fori_loop(0, i, lambda j, c: block(j, c, False), carry)
    o_ref[...] = (acc / l * _silu(fg_ref[...].astype(F32))).astype(o_ref.dtype)


def _fox(u3, cols, c, tq, tk):
    bsz, s, _ = u3.shape
    h, d = FOX_HEADS, FOX_HEADDIM
    tq, tk = min(tq, s), min(tk, s)
    assert tq == tk, "the kernel treats exactly one key block per query tile as the masked diagonal"
    q0, k0, v0, g0 = (cols["q"] // d, cols["k"] // d, cols["v"] // d, cols["f_gate"] // d)
    c_row = c.reshape(bsz, h, 1, s)
    return pl.pallas_call(
        functools.partial(_fox_kernel, tk=tk),
        grid=(bsz, h, s // tq),
        in_specs=[
            pl.BlockSpec((None, tq, d), lambda bi, hi, i: (bi, i, q0 + hi)),
            pl.BlockSpec((None, s, d), lambda bi, hi, i: (bi, 0, k0 + hi)),
            pl.BlockSpec((None, s, d), lambda bi, hi, i: (bi, 0, v0 + hi)),
            pl.BlockSpec((None, None, 1, s), lambda bi, hi, i: (bi, hi, 0, 0)),
            pl.BlockSpec((None, tq, d), lambda bi, hi, i: (bi, i, g0 + hi)),
        ],
        out_specs=pl.BlockSpec((None, tq, d), lambda bi, hi, i: (bi, i, hi)),
        out_shape=jax.ShapeDtypeStruct((bsz, s, h * d), BF16),
        compiler_params=_cparams(("parallel", "parallel", "arbitrary")),
        name="fox_attention",
    )(u3, u3, u3, c_row, u3)


def _pad_cols(w, n):
    return jnp.concatenate([w, jnp.zeros(w.shape[:-1] + (n - w.shape[-1],), w.dtype)], axis=-1)


def _even_layer(xr, bsz, j, norm_g, w_in, ig_b, fg_b, m_norm_g, conv_w, conv_b, dt_b, a_log, d_skip, s_norm_g, w_out):
    m, d = xr.shape
    s = m // bsz
    qk, mw, sw, bc = MLSTM_HEADS * MLSTM_DQK, MLSTM_HEADS * MLSTM_DV, SSD_HEADS * SSD_HEADDIM, SSD_GROUPS * SSD_STATE
    n_a = 2 * qk + 2 * mw
    n_b = n_a + 2 * MLSTM_HEADS
    n_c = n_b + 2 * sw + 2 * bc
    w_ssd = w_in[j:j + 1, :, n_b:n_c]
    w_small = _pad_cols(jnp.concatenate([w_in[j:j + 1, :, n_a:n_b], w_in[j:j + 1, :, n_c:]], axis=-1), LANES)
    cols_m = {"q": 0, "k": qk, "v": 2 * qk, "m_gate": 2 * qk + mw}
    cols_s = {"z": 0, "xs": sw, "bm": 2 * sw, "cm": 2 * sw + bc}

    hn = _rmsnorm(xr, norm_g, BF16)
    um = _matmul(hn, w_in, j, 0, n_a, BF16).reshape(bsz, s, n_a)
    us = _matmul(hn, w_ssd, 0, 0, n_c - n_b, BF16).reshape(bsz, s, n_c - n_b)
    ug = _matmul(hn, w_small, 0, 0, LANES, F32).reshape(bsz, s, LANES)
    hh = MLSTM_HEADS
    i_t = jnp.swapaxes(ug[:, :, :hh], 1, 2)
    f_t = jnp.swapaxes(ug[:, :, hh:2 * hh], 1, 2)
    dt_t = jnp.swapaxes(ug[:, :, 2 * hh:2 * hh + SSD_HEADS], 1, 2)

    mchunk = min(MLSTM_CHUNK, s)
    a, b = _mlstm_gates(i_t, f_t, ig_b, fg_b, mchunk)
    ya = _mlstm(um, cols_m, a, b, m_norm_g, mchunk)

    schunk = min(SSD_CHUNK, s)
    dt, cum = _ssd_gates(dt_t, dt_b, a_log, schunk)
    yb = _ssd(us, cols_s, conv_w, conv_b, dt, cum, d_skip, s_norm_g, schunk)

    return _out_proj(ya.reshape(m, mw), yb.reshape(m, sw), w_out, j, xr)


def _odd_layer(xr, bsz, j, norm_g, w_in, conv_w, conv_b, ln_g, ln_b, fg_b, w_out):
    m, d = xr.shape
    s = m // bsz
    ch = conv_w.shape[1]
    fw = FOX_HEADS * FOX_HEADDIM
    n_a = 3 * ch + 4 * fw
    w_small = _pad_cols(w_in[j:j + 1, :, n_a:], LANES)
    cols = {"glu_a": 0, "glu_g": ch, "c_gate": 2 * ch, "q": 3 * ch, "k": 3 * ch + fw, "v": 3 * ch + 2 * fw,
            "f_gate": 3 * ch + 3 * fw}

    hn = _rmsnorm(xr, norm_g, BF16)
    u3 = _matmul(hn, w_in, j, 0, n_a, BF16).reshape(bsz, s, n_a)
    ug = _matmul(hn, w_small, 0, 0, LANES, F32).reshape(bsz, s, LANES)
    f_t = jnp.swapaxes(ug[:, :, :FOX_HEADS], 1, 2)

    yc = _conformer(u3, cols, conv_w, conv_b, ln_g, ln_b, min(CONV_TM, s))
    c = _fox_gates(f_t, fg_b)
    yd = _fox(u3, cols, c, FOX_TQ, FOX_TK)

    return _out_proj(yc.reshape(m, ch), yd.reshape(m, fw), w_out, j, xr)


def kernel(x, e_norm_g, e_w_in, e_ig_b, e_fg_b, e_mlstm_norm_g, e_conv_w, e_conv_b, e_dt_b, e_a_log, e_d_skip, e_ssd_norm_g, e_w_out, o_norm_g, o_w_in, o_conv_w, o_conv_b, o_ln_g, o_ln_b, o_fg_b, o_w_out, final_norm_g):
    bsz, s, d = x.shape
    xr = x.reshape(bsz * s, d)
    e_w_in, e_w_out = e_w_in.astype(BF16), e_w_out.astype(BF16)
    o_w_in, o_w_out = o_w_in.astype(BF16), o_w_out.astype(BF16)
    for layer in range(DEPTH):
        j = layer // 2
        if layer % 2 == 0:
            xr = _even_layer(xr, bsz, j, e_norm_g[j], e_w_in, e_ig_b[j], e_fg_b[j], e_mlstm_norm_g[j], e_conv_w[j],
                             e_conv_b[j], e_dt_b[j], e_a_log[j], e_d_skip[j], e_ssd_norm_g[j], e_w_out)
        else:
            xr = _odd_layer(xr, bsz, j, o_norm_g[j], o_w_in, o_conv_w[j], o_conv_b[j], o_ln_g[j], o_ln_b[j],
                            o_fg_b[j], o_w_out)
    return _rmsnorm(xr, final_norm_g, F32).reshape(bsz, s, d)
```

```python
import functools
import math

import jax
import jax.numpy as jnp
from jax import lax
from jax.experimental import pallas as pl
from jax.experimental.pallas import tpu as pltpu

F32 = jnp.float32
BF16 = jnp.bfloat16

EPS = 1e-6
DEPTH = 4
LANES = 128
SUBLANES = 8
MLSTM_HEADS = 8
MLSTM_DQK = 256
MLSTM_DV = 512
MLSTM_CHUNK = 256
SSD_HEADS = 64
SSD_HEADDIM = 64
SSD_GROUPS = 8
SSD_REP = SSD_HEADS // SSD_GROUPS
SSD_STATE = 128
SSD_TAPS = 4
SSD_CHUNK = 128
SSD_GW = SSD_REP * SSD_HEADDIM
CONV_TAPS = 31
CONV_HALO = 32
CONV_TM = 256
FOX_HEADS = 32
FOX_HEADDIM = 128
FOX_TQ = 1024
FOX_TK = 1024
VMEM_LIMIT = 56 * 1024 * 1024

NEG_INF = float("-inf")
LOG2E = math.log2(math.e)


def _cparams(sem):
    return pltpu.CompilerParams(dimension_semantics=sem, vmem_limit_bytes=VMEM_LIMIT)


def _sigmoid(x):
    return 0.5 * jnp.tanh(0.5 * x) + 0.5


def _silu(x):
    return x * _sigmoid(x)


def _softplus(x):
    return jnp.maximum(x, 0.0) + jnp.log1p(jnp.exp(-jnp.abs(x)))


def _log_sigmoid(x):
    return jnp.minimum(x, 0.0) - jnp.log1p(jnp.exp(-jnp.abs(x)))


def _dot(a, b):
    return jnp.dot(a, b, preferred_element_type=F32)


def _dot_nt(a, b):
    return lax.dot_general(a, b, (((1,), (1,)), ((), ())), preferred_element_type=F32)


def _dot_tn(a, b):
    return lax.dot_general(a, b, (((0,), (0,)), ((), ())), preferred_element_type=F32)


def _split3(x):
    hi = x.astype(BF16)
    r1 = x - hi.astype(F32)
    mid = r1.astype(BF16)
    lo = (r1 - mid.astype(F32)).astype(BF16)
    return hi, mid, lo


def _dot01(x, m01):
    hi, mid, lo = _split3(x)
    return _dot(hi, m01) + _dot(mid, m01) + _dot(lo, m01)


def _triu01(n):
    r = lax.broadcasted_iota(jnp.int32, (n, n), 0)
    c = lax.broadcasted_iota(jnp.int32, (n, n), 1)
    return jnp.where(r <= c, 1.0, 0.0).astype(BF16)


def _causal_mask(n):
    r = lax.broadcasted_iota(jnp.int32, (n, n), 0)
    c = lax.broadcasted_iota(jnp.int32, (n, n), 1)
    return c <= r


def _rmsnorm_kernel(x_ref, g_ref, o_ref):
    x = x_ref[...]
    y = x * lax.rsqrt(jnp.mean(x * x, axis=-1, keepdims=True) + EPS)
    o_ref[...] = (y * g_ref[...]).astype(o_ref.dtype)


def _rmsnorm(x, g, out_dtype, tm=256):
    m, d = x.shape
    return pl.pallas_call(
        _rmsnorm_kernel,
        grid=(m // tm,),
        in_specs=[pl.BlockSpec((tm, d), lambda i: (i, 0)), pl.BlockSpec((1, d), lambda i: (0, 0))],
        out_specs=pl.BlockSpec((tm, d), lambda i: (i, 0)),
        out_shape=jax.ShapeDtypeStruct((m, d), out_dtype),
        compiler_params=_cparams(("parallel",)),
        name="rmsnorm",
    )(x, g.reshape(1, d))


def _matmul_kernel(x_ref, w_ref, o_ref, wb_scr):
    @pl.when(pl.program_id(1) == 0)
    def _():
        wb_scr[...] = w_ref[...].astype(BF16)

    o_ref[...] = _dot(x_ref[...], wb_scr[...]).astype(o_ref.dtype)


def _matmul(x, w, layer, col0, n, out_dtype, tm=1024, tn=512):
    m, k = x.shape
    tm, tn = min(tm, m), min(tn, n)
    c0 = col0 // tn
    return pl.pallas_call(
        _matmul_kernel,
        grid=(n // tn, m // tm),
        in_specs=[pl.BlockSpec((tm, k), lambda j, i: (i, 0)),
                  pl.BlockSpec((None, k, tn), lambda j, i: (layer, 0, c0 + j))],
        out_specs=pl.BlockSpec((tm, tn), lambda j, i: (i, j)),
        out_shape=jax.ShapeDtypeStruct((m, n), out_dtype),
        scratch_shapes=[pltpu.VMEM((k, tn), BF16)],
        compiler_params=_cparams(("parallel", "arbitrary")),
        name="in_proj",
    )(x, w)


def _out_proj_kernel(ya_ref, yb_ref, wa_ref, wb_ref, x_ref, o_ref):
    o_ref[...] = x_ref[...] + _dot(ya_ref[...], wa_ref[...]) + _dot(yb_ref[...], wb_ref[...])


def _out_proj(ya, yb, w, layer, xres, tm=512, tn=512):
    m, k = ya.shape
    n = w.shape[-1]
    tm, tn = min(tm, m), min(tn, n)
    return pl.pallas_call(
        _out_proj_kernel,
        grid=(m // tm, n // tn),
        in_specs=[pl.BlockSpec((tm, k), lambda i, j: (i, 0)), pl.BlockSpec((tm, k), lambda i, j: (i, 0)),
                  pl.BlockSpec((None, k, tn), lambda i, j: (layer, 0, j)),
                  pl.BlockSpec((None, k, tn), lambda i, j: (layer, 1, j)),
                  pl.BlockSpec((tm, tn), lambda i, j: (i, j))],
        out_specs=pl.BlockSpec((tm, tn), lambda i, j: (i, j)),
        out_shape=jax.ShapeDtypeStruct((m, n), F32),
        compiler_params=_cparams(("parallel", "arbitrary")),
        name="out_proj",
    )(ya, yb, w, w, xres)


def _mlstm_gates_kernel(i_ref, f_ref, ib_ref, fb_ref, a_ref, b_ref, *, chunk):
    s = i_ref.shape[-1]
    li = i_ref[...] + ib_ref[...]
    lf = _log_sigmoid(f_ref[...] + fb_ref[...])
    tri = _triu01(chunk)
    for c in range(s // chunk):
        sl = slice(c * chunk, (c + 1) * chunk)
        b = _dot01(lf[:, sl], tri)
        b_ref[:, sl] = b
        a_ref[:, sl] = li[:, sl] - b


def _mlstm_gates(i_t, f_t, ig_b, fg_b, chunk):
    bsz, h, s = i_t.shape
    blk = pl.BlockSpec((None, h, s), lambda b: (b, 0, 0))
    bias = pl.BlockSpec((h, 1), lambda b: (0, 0))
    return pl.pallas_call(
        functools.partial(_mlstm_gates_kernel, chunk=chunk),
        grid=(bsz,),
        in_specs=[blk, blk, bias, bias],
        out_specs=[blk, blk],
        out_shape=[jax.ShapeDtypeStruct((bsz, h, s), F32)] * 2,
        compiler_params=_cparams(("parallel",)),
        name="mlstm_gates",
    )(i_t, f_t, ig_b.reshape(h, 1), fg_b.reshape(h, 1))


def _ssd_gates_kernel(dt_ref, dtb_ref, alog_ref, dto_ref, cum_ref, *, chunk):
    s = dt_ref.shape[-1]
    dt = _softplus(dt_ref[...] + dtb_ref[...])
    a = dt * (-jnp.exp(alog_ref[...]))
    dto_ref[...] = dt
    tri = _triu01(chunk)
    for c in range(s // chunk):
        sl = slice(c * chunk, (c + 1) * chunk)
        cum_ref[:, sl] = _dot01(a[:, sl], tri)


def _ssd_gates(dt_t, dt_b, a_log, chunk):
    bsz, h, s = dt_t.shape
    blk = pl.BlockSpec((None, h, s), lambda b: (b, 0, 0))
    par = pl.BlockSpec((h, 1), lambda b: (0, 0))
    return pl.pallas_call(
        functools.partial(_ssd_gates_kernel, chunk=chunk),
        grid=(bsz,),
        in_specs=[blk, par, par],
        out_specs=[blk, blk],
        out_shape=[jax.ShapeDtypeStruct((bsz, h, s), F32)] * 2,
        compiler_params=_cparams(("parallel",)),
        name="ssd_gates",
    )(dt_t, dt_b.reshape(h, 1), a_log.reshape(h, 1))


def _fox_gates_kernel(f_ref, fb_ref, c_ref, *, chunk):
    s = f_ref.shape[-1]
    lf = _log_sigmoid(f_ref[...] + fb_ref[...])
    tri = _triu01(chunk)
    carry = jnp.zeros((f_ref.shape[0], 1), F32)
    for c in range(s // chunk):
        sl = slice(c * chunk, (c + 1) * chunk)
        cc = _dot01(lf[:, sl], tri) + carry
        c_ref[:, sl] = cc
        carry = cc[:, chunk - 1:chunk]


def _fox_gates(f_t, fg_b, chunk=256):
    bsz, h, s = f_t.shape
    chunk = min(chunk, s)
    blk = pl.BlockSpec((None, h, s), lambda b: (b, 0, 0))
    return pl.pallas_call(
        functools.partial(_fox_gates_kernel, chunk=chunk),
        grid=(bsz,),
        in_specs=[blk, pl.BlockSpec((h, 1), lambda b: (0, 0))],
        out_specs=blk,
        out_shape=jax.ShapeDtypeStruct((bsz, h, s), F32),
        compiler_params=_cparams(("parallel",)),
        name="fox_gates",
    )(f_t, fg_b.reshape(h, 1))


def _mlstm_kernel(q_ref, k_ref, v_ref, mg_ref, arow_ref, acol_ref, bcol_ref, gain_ref, o_ref, c_scr, m_scr):
    L, dqk = q_ref.shape
    dv = v_ref.shape[-1]
    scale = dqk ** -0.5

    @pl.when(pl.program_id(2) == 0)
    def _():
        c_scr[...] = jnp.zeros_like(c_scr)
        m_scr[...] = jnp.zeros_like(m_scr)

    q = q_ref[...]
    k = k_ref[...]
    lane = lax.broadcasted_iota(jnp.int32, (L, LANES), 1)
    vext = jnp.concatenate([v_ref[...], jnp.where(lane == 0, 1.0, 0.0).astype(BF16)], axis=1)
    a_row = arow_ref[...]
    a_col = acol_ref[...]
    b_col = bcol_ref[...]
    m_prev = m_scr[0:1, 0:1]

    dmat = jnp.where(_causal_mask(L), b_col + a_row, NEG_INF)
    inter = b_col + m_prev
    m_t = jnp.maximum(inter, jnp.max(dmat, axis=1, keepdims=True))
    w = jnp.exp(dmat - m_t)
    sc = (_dot_nt(q, k) * scale * w).astype(BF16)
    g_inter = jnp.exp(inter - m_t) * scale
    numden = _dot(sc, vext) + g_inter * _dot(q, c_scr[...].astype(BF16))
    num = numden[:, :dv]
    den = numden[:, dv:dv + 1]
    h = num / jnp.maximum(jnp.abs(den), jnp.exp(-m_t))

    hn = h * lax.rsqrt(jnp.mean(h * h, axis=-1, keepdims=True) + EPS) * gain_ref[...]
    o_ref[...] = (hn * _silu(mg_ref[...].astype(F32))).astype(o_ref.dtype)

    b_last = b_col[L - 1:L, :]
    gk = b_last + a_col
    m_new = jnp.maximum(b_last + m_prev, jnp.max(gk, axis=0, keepdims=True))
    decay = jnp.exp(b_last + m_prev - m_new)
    kw = (k.astype(F32) * jnp.exp(gk - m_new)).astype(BF16)
    c_scr[...] = decay * c_scr[...] + _dot_tn(kw, vext)
    m_scr[...] = jnp.broadcast_to(m_new, m_scr.shape)


def _mlstm(u3, cols, a, b, gain, chunk):
    bsz, s, _ = u3.shape
    h, dqk, dv = MLSTM_HEADS, MLSTM_DQK, MLSTM_DV
    q0, k0, v0, g0 = (cols["q"] // dqk, cols["k"] // dqk, cols["v"] // dv, cols["m_gate"] // dv)
    a_row = a.reshape(bsz, h, 1, s)
    a_col = a.reshape(bsz, h, s, 1)
    b_col = b.reshape(bsz, h, s, 1)
    col_spec = pl.BlockSpec((None, None, chunk, 1), lambda bi, hi, ci: (bi, hi, ci, 0))
    return pl.pallas_call(
        _mlstm_kernel,
        grid=(bsz, h, s // chunk),
        in_specs=[
            pl.BlockSpec((None, chunk, dqk), lambda bi, hi, ci: (bi, ci, q0 + hi)),
            pl.BlockSpec((None, chunk, dqk), lambda bi, hi, ci: (bi, ci, k0 + hi)),
            pl.BlockSpec((None, chunk, dv), lambda bi, hi, ci: (bi, ci, v0 + hi)),
            pl.BlockSpec((None, chunk, dv), lambda bi, hi, ci: (bi, ci, g0 + hi)),
            pl.BlockSpec((None, None, 1, chunk), lambda bi, hi, ci: (bi, hi, 0, ci)),
            col_spec, col_spec,
            pl.BlockSpec((1, dv), lambda bi, hi, ci: (0, hi)),
        ],
        out_specs=pl.BlockSpec((None, chunk, dv), lambda bi, hi, ci: (bi, ci, hi)),
        out_shape=jax.ShapeDtypeStruct((bsz, s, h * dv), BF16),
        scratch_shapes=[pltpu.VMEM((dqk, dv + LANES), F32), pltpu.VMEM((SUBLANES, LANES), F32)],
        compiler_params=_cparams(("parallel", "parallel", "arbitrary")),
        name="mlstm",
    )(u3, u3, u3, u3, a_row, a_col, b_col, gain.reshape(1, h * dv))


def _ssd_kernel(xs_ref, bm_ref, cm_ref, z_ref, wx_ref, wb_ref, wc_ref, bx_ref, bb_ref, bc_ref,
                dt_ref, cum_ref, dtc_ref, cumc_ref, dskip_ref, gain_ref, o_ref,
                st_scr, xbuf, bbuf, cbuf):
    L = xs_ref.shape[0]
    P, R = SSD_HEADDIM, SSD_REP
    first = pl.program_id(2) == 0

    def conv(src_ref, buf, w_ref, b_ref):
        @pl.when(first)
        def _():
            buf[0:SUBLANES, :] = jnp.zeros((SUBLANES, buf.shape[1]), F32)

        @pl.when(jnp.logical_not(first))
        def _():
            buf[0:SUBLANES, :] = buf[L:L + SUBLANES, :]

        buf[SUBLANES:SUBLANES + L, :] = src_ref[...].astype(F32)
        acc = b_ref[...]
        for t in range(SSD_TAPS):
            off = SUBLANES - (SSD_TAPS - 1) + t
            acc = acc + w_ref[t:t + 1, :] * buf[off:off + L, :]
        return _silu(acc)

    @pl.when(first)
    def _():
        st_scr[...] = jnp.zeros_like(st_scr)

    xc = conv(xs_ref, xbuf, wx_ref, bx_ref)
    bc = conv(bm_ref, bbuf, wb_ref, bb_ref).astype(BF16)
    cc = conv(cm_ref, cbuf, wc_ref, bc_ref).astype(BF16)

    cb = _dot_nt(cc, bc)
    st = st_scr[...]
    cs = _dot(cc, st.astype(BF16))
    mask = _causal_mask(L)
    low = lax.broadcasted_iota(jnp.int32, (1, 2 * P), 1) < P

    cum_cols = cumc_ref[...]
    dt_cols = dtc_ref[...]

    def cum_col(r):
        return cum_cols[:, r:r + 1]

    def dt_col(r):
        return dt_cols[:, r:r + 1]

    def head_weights(r):
        lmat = jnp.exp(jnp.where(mask, cum_col(r) - cum_ref[r:r + 1, :], NEG_INF))
        return (cb * lmat * dt_ref[r:r + 1, :]).astype(BF16)

    y_parts, xw_parts, dec_parts = [], [], []
    for pr in range(R // 2):
        r0, r1 = 2 * pr, 2 * pr + 1
        sl = slice(pr * 2 * P, (pr + 1) * 2 * P)
        xp = xc[:, sl]
        c0, c1 = cum_col(r0), cum_col(r1)
        wts = jnp.concatenate([head_weights(r0), head_weights(r1)], axis=1)
        xbd = jnp.concatenate([jnp.where(low, xp, 0.0), jnp.where(low, 0.0, xp)], axis=0)
        y_parts.append(_dot(wts, xbd.astype(BF16)) + cs[:, sl] * jnp.where(low, jnp.exp(c0), jnp.exp(c1)))
        l0, l1 = c0[L - 1:L, :], c1[L - 1:L, :]
        xw_parts.append(xp * jnp.where(low, jnp.exp(l0 - c0) * dt_col(r0), jnp.exp(l1 - c1) * dt_col(r1)))
        dec_parts.append(jnp.where(low, jnp.exp(l0), jnp.exp(l1)))

    xw = jnp.concatenate(xw_parts, axis=1).astype(BF16)
    st_scr[...] = st * jnp.concatenate(dec_parts, axis=1) + _dot_tn(bc, xw)

    y = jnp.concatenate(y_parts, axis=1) + dskip_ref[...] * xc
    y = y * _silu(z_ref[...].astype(F32))
    yn = y * lax.rsqrt(jnp.mean(y * y, axis=-1, keepdims=True) + EPS) * gain_ref[...]
    o_ref[...] = yn.astype(o_ref.dtype)


def _ssd(u3, cols, conv_w, conv_b, dt, cum, d_skip, gain, chunk):
    bsz, s, _ = u3.shape
    g, r, gw, n = SSD_GROUPS, SSD_REP, SSD_GW, SSD_STATE
    width = g * gw
    x0, z0 = cols["xs"] // gw, cols["z"] // gw
    b0, c0 = cols["bm"] // n, cols["cm"] // n
    conv_b2 = conv_b.reshape(1, -1)
    dskip = jnp.repeat(d_skip, SSD_HEADDIM).reshape(1, width)
    dt_c = jnp.swapaxes(dt.reshape(bsz, g, r, s), 2, 3)
    cum_c = jnp.swapaxes(cum.reshape(bsz, g, r, s), 2, 3)
    rows_spec = pl.BlockSpec((None, r, chunk), lambda bi, gi, ci: (bi, gi, ci))
    cols_spec = pl.BlockSpec((None, None, chunk, r), lambda bi, gi, ci: (bi, gi, ci, 0))
    return pl.pallas_call(
        _ssd_kernel,
        grid=(bsz, g, s // chunk),
        in_specs=[
            pl.BlockSpec((None, chunk, gw), lambda bi, gi, ci: (bi, ci, x0 + gi)),
            pl.BlockSpec((None, chunk, n), lambda bi, gi, ci: (bi, ci, b0 + gi)),
            pl.BlockSpec((None, chunk, n), lambda bi, gi, ci: (bi, ci, c0 + gi)),
            pl.BlockSpec((None, chunk, gw), lambda bi, gi, ci: (bi, ci, z0 + gi)),
            pl.BlockSpec((SSD_TAPS, gw), lambda bi, gi, ci: (0, gi)),
            pl.BlockSpec((SSD_TAPS, n), lambda bi, gi, ci: (0, width // n + gi)),
            pl.BlockSpec((SSD_TAPS, n), lambda bi, gi, ci: (0, width // n + g + gi)),
            pl.BlockSpec((1, gw), lambda bi, gi, ci: (0, gi)),
            pl.BlockSpec((1, n), lambda bi, gi, ci: (0, width // n + gi)),
            pl.BlockSpec((1, n), lambda bi, gi, ci: (0, width // n + g + gi)),
            rows_spec, rows_spec, cols_spec, cols_spec,
            pl.BlockSpec((1, gw), lambda bi, gi, ci: (0, gi)),
            pl.BlockSpec((1, gw), lambda bi, gi, ci: (0, gi)),
        ],
        out_specs=pl.BlockSpec((None, chunk, gw), lambda bi, gi, ci: (bi, ci, gi)),
        out_shape=jax.ShapeDtypeStruct((bsz, s, width), BF16),
        scratch_shapes=[pltpu.VMEM((n, gw), F32),
                        pltpu.VMEM((chunk + SUBLANES, gw), F32), pltpu.VMEM((chunk + SUBLANES, n), F32),
                        pltpu.VMEM((chunk + SUBLANES, n), F32)],
        compiler_params=_cparams(("parallel", "parallel", "arbitrary")),
        name="ssd",
    )(u3, u3, u3, u3, conv_w, conv_w, conv_w, conv_b2, conv_b2, conv_b2,
      dt, cum, dt_c, cum_c, dskip, gain.reshape(1, width))


def _conformer_kernel(a_ref, g_ref, cg_ref, w_ref, b_ref, lng_ref, lnb_ref, o_ref, buf, cv):
    tm, ch = a_ref.shape
    halo = CONV_HALO
    first = pl.program_id(1) == 0

    @pl.when(first)
    def _():
        buf[0:halo, :] = jnp.zeros((halo, ch), F32)

    @pl.when(jnp.logical_not(first))
    def _():
        buf[0:halo, :] = buf[tm:tm + halo, :]

    buf[halo:halo + tm, :] = a_ref[...].astype(F32) * _sigmoid(g_ref[...].astype(F32))

    rs, cw = 128, 128
    off0 = halo - (CONV_TAPS - 1)

    def strip(idx, carry):
        c0 = pl.multiple_of(idx * cw, cw)
        for r0 in range(0, tm, rs):
            acc = jnp.broadcast_to(b_ref[:, pl.ds(c0, cw)], (rs, cw))
            for p in range(SUBLANES):
                ext = rs if p == 0 else rs + SUBLANES
                tp = None
                for t in range(CONV_TAPS):
                    if (off0 + t) % SUBLANES != p:
                        continue
                    base = r0 + off0 + t - p
                    term = w_ref[t:t + 1, pl.ds(c0, cw)] * buf[base:base + ext, pl.ds(c0, cw)]
                    tp = term if tp is None else tp + term
                acc = acc + (tp if p == 0 else tp[p:p + rs])
            cv[r0:r0 + rs, pl.ds(c0, cw)] = acc
        return carry

    lax.fori_loop(0, ch // cw, strip, 0)

    rb = 2 * SUBLANES

    def norm_rows(idx, carry):
        r0 = pl.multiple_of(idx * rb, rb)
        c = cv[pl.ds(r0, rb), :]
        mu = jnp.mean(c, axis=-1, keepdims=True)
        xc = c - mu
        y = xc * lax.rsqrt(jnp.mean(xc * xc, axis=-1, keepdims=True) + EPS)
        y = y * lng_ref[...] + lnb_ref[...]
        gate = _silu(cg_ref[pl.ds(r0, rb), :].astype(F32))
        o_ref[pl.ds(r0, rb), :] = (_silu(y) * gate).astype(o_ref.dtype)
        return carry

    lax.fori_loop(0, tm // rb, norm_rows, 0, unroll=2)


def _conformer(u3, cols, conv_w, conv_b, ln_g, ln_b, tm):
    bsz, s, _ = u3.shape
    ch = conv_w.shape[1]
    a0, g0, c0 = cols["glu_a"] // ch, cols["glu_g"] // ch, cols["c_gate"] // ch
    w_pad = jnp.concatenate([conv_w, jnp.zeros((CONV_HALO - CONV_TAPS, ch), conv_w.dtype)], axis=0)
    row = pl.BlockSpec((1, ch), lambda bi, i: (0, 0))
    return pl.pallas_call(
        _conformer_kernel,
        grid=(bsz, s // tm),
        in_specs=[
            pl.BlockSpec((None, tm, ch), lambda bi, i: (bi, i, a0)),
            pl.BlockSpec((None, tm, ch), lambda bi, i: (bi, i, g0)),
            pl.BlockSpec((None, tm, ch), lambda bi, i: (bi, i, c0)),
            pl.BlockSpec((CONV_HALO, ch), lambda bi, i: (0, 0)),
            row, row, row,
        ],
        out_specs=pl.BlockSpec((None, tm, ch), lambda bi, i: (bi, i, 0)),
        out_shape=jax.ShapeDtypeStruct((bsz, s, ch), BF16),
        scratch_shapes=[pltpu.VMEM((CONV_HALO + tm, ch), F32), pltpu.VMEM((tm, ch), F32)],
        compiler_params=_cparams(("parallel", "arbitrary")),
        name="conformer_conv",
    )(u3, u3, u3, w_pad, conv_b.reshape(1, ch), ln_g.reshape(1, ch), ln_b.reshape(1, ch))


def _fox_kernel(q_ref, k_ref, v_ref, c_ref, fg_ref, o_ref, *, tk):
    tq, d = q_ref.shape
    i = pl.program_id(2)
    q2 = (q_ref[...].astype(F32) * (d ** -0.5 * LOG2E)).astype(BF16)

    def block(j, carry, masked):
        m, l, acc = carry
        k0 = pl.multiple_of(j * tk, tk)
        s = _dot_nt(q2, k_ref[pl.ds(k0, tk), :]) - c_ref[:, pl.ds(k0, tk)] * LOG2E
        if masked:
            qpos = i * tq + lax.broadcasted_iota(jnp.int32, (tq, tk), 0)
            kpos = k0 + lax.broadcasted_iota(jnp.int32, (tq, tk), 1)
            s = jnp.where(kpos <= qpos, s, NEG_INF)
        m_new = jnp.maximum(m, jnp.max(s, axis=-1, keepdims=True))
        alpha = jnp.exp2(m - m_new)
        p = jnp.exp2(s - m_new)
        l = alpha * l + jnp.sum(p, axis=-1, keepdims=True)
        acc = alpha * acc + _dot(p.astype(BF16), v_ref[pl.ds(k0, tk), :])
        return m_new, l, acc

    carry = (jnp.full((tq, 1), NEG_INF, F32), jnp.zeros((tq, 1), F32), jnp.zeros((tq, d), F32))
    carry = block(i, carry, True)
    _, l, acc = lax.fori_loop(0, i, lambda j, c: block(j, c, False), carry)
    o_ref[...] = (acc / l * _silu(fg_ref[...].astype(F32))).astype(o_ref.dtype)


def _fox(u3, cols, c, tq, tk):
    bsz, s, _ = u3.shape
    h, d = FOX_HEADS, FOX_HEADDIM
    tq, tk = min(tq, s), min(tk, s)
    assert tq == tk, "the kernel treats exactly one key block per query tile as the masked diagonal"
    q0, k0, v0, g0 = (cols["q"] // d, cols["k"] // d, cols["v"] // d, cols["f_gate"] // d)
    c_row = c.reshape(bsz, h, 1, s)
    return pl.pallas_call(
        functools.partial(_fox_kernel, tk=tk),
        grid=(bsz, h, s // tq),
        in_specs=[
            pl.BlockSpec((None, tq, d), lambda bi, hi, i: (bi, i, q0 + hi)),
            pl.BlockSpec((None, s, d), lambda bi, hi, i: (bi, 0, k0 + hi)),
            pl.BlockSpec((None, s, d), lambda bi, hi, i: (bi, 0, v0 + hi)),
            pl.BlockSpec((None, None, 1, s), lambda bi, hi, i: (bi, hi, 0, 0)),
            pl.BlockSpec((None, tq, d), lambda bi, hi, i: (bi, i, g0 + hi)),
        ],
        out_specs=pl.BlockSpec((None, tq, d), lambda bi, hi, i: (bi, i, hi)),
        out_shape=jax.ShapeDtypeStruct((bsz, s, h * d), BF16),
        compiler_params=_cparams(("parallel", "parallel", "arbitrary")),
        name="fox_attention",
    )(u3, u3, u3, c_row, u3)


def _pad_cols(w, n):
    return jnp.concatenate([w, jnp.zeros(w.shape[:-1] + (n - w.shape[-1],), w.dtype)], axis=-1)


def _even_layer(xr, bsz, j, norm_g, w_in, ig_b, fg_b, m_norm_g, conv_w, conv_b, dt_b, a_log, d_skip, s_norm_g, w_out):
    m, d = xr.shape
    s = m // bsz
    qk, mw, sw, bc = MLSTM_HEADS * MLSTM_DQK, MLSTM_HEADS * MLSTM_DV, SSD_HEADS * SSD_HEADDIM, SSD_GROUPS * SSD_STATE
    n_a = 2 * qk + 2 * mw
    n_b = n_a + 2 * MLSTM_HEADS
    n_c = n_b + 2 * sw + 2 * bc
    w_ssd = w_in[j:j + 1, :, n_b:n_c]
    w_small = _pad_cols(jnp.concatenate([w_in[j:j + 1, :, n_a:n_b], w_in[j:j + 1, :, n_c:]], axis=-1), LANES)
    cols_m = {"q": 0, "k": qk, "v": 2 * qk, "m_gate": 2 * qk + mw}
    cols_s = {"z": 0, "xs": sw, "bm": 2 * sw, "cm": 2 * sw + bc}

    hn = _rmsnorm(xr, norm_g, BF16)
    um = _matmul(hn, w_in, j, 0, n_a, BF16).reshape(bsz, s, n_a)
    us = _matmul(hn, w_ssd, 0, 0, n_c - n_b, BF16).reshape(bsz, s, n_c - n_b)
    ug = _matmul(hn, w_small, 0, 0, LANES, F32).reshape(bsz, s, LANES)
    hh = MLSTM_HEADS
    i_t = jnp.swapaxes(ug[:, :, :hh], 1, 2)
    f_t = jnp.swapaxes(ug[:, :, hh:2 * hh], 1, 2)
    dt_t = jnp.swapaxes(ug[:, :, 2 * hh:2 * hh + SSD_HEADS], 1, 2)

    mchunk = min(MLSTM_CHUNK, s)
    a, b = _mlstm_gates(i_t, f_t, ig_b, fg_b, mchunk)
    ya = _mlstm(um, cols_m, a, b, m_norm_g, mchunk)

    schunk = min(SSD_CHUNK, s)
    dt, cum = _ssd_gates(dt_t, dt_b, a_log, schunk)
    yb = _ssd(us, cols_s, conv_w, conv_b, dt, cum, d_skip, s_norm_g, schunk)

    return _out_proj(ya.reshape(m, mw), yb.reshape(m, sw), w_out, j, xr)


def _odd_layer(xr, bsz, j, norm_g, w_in, conv_w, conv_b, ln_g, ln_b, fg_b, w_out):
    m, d = xr.shape
    s = m // bsz
    ch = conv_w.shape[1]
    fw = FOX_HEADS * FOX_HEADDIM
    n_a = 3 * ch + 4 * fw
    w_small = _pad_cols(w_in[j:j + 1, :, n_a:], LANES)
    cols = {"glu_a": 0, "glu_g": ch, "c_gate": 2 * ch, "q": 3 * ch, "k": 3 * ch + fw, "v": 3 * ch + 2 * fw,
            "f_gate": 3 * ch + 3 * fw}

    hn = _rmsnorm(xr, norm_g, BF16)
    u3 = _matmul(hn, w_in, j, 0, n_a, BF16).reshape(bsz, s, n_a)
    ug = _matmul(hn, w_small, 0, 0, LANES, F32).reshape(bsz, s, LANES)
    f_t = jnp.swapaxes(ug[:, :, :FOX_HEADS], 1, 2)

    yc = _conformer(u3, cols, conv_w, conv_b, ln_g, ln_b, min(CONV_TM, s))
    c = _fox_gates(f_t, fg_b)
    yd = _fox(u3, cols, c, FOX_TQ, FOX_TK)

    return _out_proj(yc.reshape(m, ch), yd.reshape(m, fw), w_out, j, xr)


def kernel(x, e_norm_g, e_w_in, e_ig_b, e_fg_b, e_mlstm_norm_g, e_conv_w, e_conv_b, e_dt_b, e_a_log, e_d_skip, e_ssd_norm_g, e_w_out, o_norm_g, o_w_in, o_conv_w, o_conv_b, o_ln_g, o_ln_b, o_fg_b, o_w_out, final_norm_g):
    bsz, s, d = x.shape
    xr = x.reshape(bsz * s, d)
    e_w_out, o_w_out = e_w_out.astype(BF16), o_w_out.astype(BF16)
    for layer in range(DEPTH):
        j = layer // 2
        if layer % 2 == 0:
            xr = _even_layer(xr, bsz, j, e_norm_g[j], e_w_in, e_ig_b[j], e_fg_b[j], e_mlstm_norm_g[j], e_conv_w[j],
                             e_conv_b[j], e_dt_b[j], e_a_log[j], e_d_skip[j], e_ssd_norm_g[j], e_w_out)
        else:
            xr = _odd_layer(xr, bsz, j, o_norm_g[j], o_w_in, o_conv_w[j], o_conv_b[j], o_ln_g[j], o_ln_b[j],
                            o_fg_b[j], o_w_out)
    return _rmsnorm(xr, final_norm_g, F32).reshape(bsz, s, d)
```

```python
import functools
import math

import jax
import jax.numpy as jnp
from jax import lax
from jax.experimental import pallas as pl
from jax.experimental.pallas import tpu as pltpu

F32 = jnp.float32
BF16 = jnp.bfloat16

EPS = 1e-6
DEPTH = 4
LANES = 128
SUBLANES = 8
MLSTM_HEADS = 8
MLSTM_DQK = 256
MLSTM_DV = 512
MLSTM_CHUNK = 256
SSD_HEADS = 64
SSD_HEADDIM = 64
SSD_GROUPS = 8
SSD_REP = SSD_HEADS // SSD_GROUPS
SSD_STATE = 128
SSD_TAPS = 4
SSD_CHUNK = 128
SSD_GW = SSD_REP * SSD_HEADDIM
CONV_TAPS = 31
CONV_HALO = 32
CONV_TM = 256
FOX_HEADS = 32
FOX_HEADDIM = 128
FOX_TQ = 1024
FOX_TK = 1024
VMEM_LIMIT = 56 * 1024 * 1024

NEG_INF = float("-inf")
LOG2E = math.log2(math.e)


def _cparams(sem):
    return pltpu.CompilerParams(dimension_semantics=sem, vmem_limit_bytes=VMEM_LIMIT)


def _sigmoid(x):
    return 0.5 * jnp.tanh(0.5 * x) + 0.5


def _silu(x):
    return x * _sigmoid(x)


def _softplus(x):
    return jnp.maximum(x, 0.0) + jnp.log1p(jnp.exp(-jnp.abs(x)))


def _log_sigmoid(x):
    return jnp.minimum(x, 0.0) - jnp.log1p(jnp.exp(-jnp.abs(x)))


def _dot(a, b):
    return jnp.dot(a, b, preferred_element_type=F32)


def _dot_nt(a, b):
    return lax.dot_general(a, b, (((1,), (1,)), ((), ())), preferred_element_type=F32)


def _dot_tn(a, b):
    return lax.dot_general(a, b, (((0,), (0,)), ((), ())), preferred_element_type=F32)


def _split3(x):
    hi = x.astype(BF16)
    r1 = x - hi.astype(F32)
    mid = r1.astype(BF16)
    lo = (r1 - mid.astype(F32)).astype(BF16)
    return hi, mid, lo


def _dot01(x, m01):
    hi, mid, lo = _split3(x)
    return _dot(hi, m01) + _dot(mid, m01) + _dot(lo, m01)


def _triu01(n):
    r = lax.broadcasted_iota(jnp.int32, (n, n), 0)
    c = lax.broadcasted_iota(jnp.int32, (n, n), 1)
    return jnp.where(r <= c, 1.0, 0.0).astype(BF16)


def _causal_mask(n):
    r = lax.broadcasted_iota(jnp.int32, (n, n), 0)
    c = lax.broadcasted_iota(jnp.int32, (n, n), 1)
    return c <= r


def _rmsnorm_kernel(x_ref, g_ref, o_ref):
    x = x_ref[...]
    y = x * lax.rsqrt(jnp.mean(x * x, axis=-1, keepdims=True) + EPS)
    o_ref[...] = (y * g_ref[...]).astype(o_ref.dtype)


def _rmsnorm(x, g, out_dtype, tm=256):
    m, d = x.shape
    return pl.pallas_call(
        _rmsnorm_kernel,
        grid=(m // tm,),
        in_specs=[pl.BlockSpec((tm, d), lambda i: (i, 0)), pl.BlockSpec((1, d), lambda i: (0, 0))],
        out_specs=pl.BlockSpec((tm, d), lambda i: (i, 0)),
        out_shape=jax.ShapeDtypeStruct((m, d), out_dtype),
        compiler_params=_cparams(("parallel",)),
        name="rmsnorm",
    )(x, g.reshape(1, d))


def _matmul_kernel(x_ref, wt_ref, o_ref, wb_scr):
    @pl.when(pl.program_id(1) == 0)
    def _():
        wb_scr[...] = wt_ref[...].astype(BF16)

    o_ref[...] = _dot_nt(x_ref[...], wb_scr[...]).astype(o_ref.dtype)


def _matmul(x, wt, layer, row0, n, out_dtype, tm=1024, tn=512):
    m, k = x.shape
    tm, tn = min(tm, m), min(tn, n)
    r0 = row0 // tn
    return pl.pallas_call(
        _matmul_kernel,
        grid=(n // tn, m // tm),
        in_specs=[pl.BlockSpec((tm, k), lambda j, i: (i, 0)),
                  pl.BlockSpec((None, tn, k), lambda j, i: (layer, r0 + j, 0))],
        out_specs=pl.BlockSpec((tm, tn), lambda j, i: (i, j)),
        out_shape=jax.ShapeDtypeStruct((m, n), out_dtype),
        scratch_shapes=[pltpu.VMEM((tn, k), BF16)],
        compiler_params=_cparams(("parallel", "arbitrary")),
        name="in_proj",
    )(x, wt)


def _out_proj_kernel(ya_ref, yb_ref, wa_ref, wb_ref, x_ref, o_ref):
    o_ref[...] = x_ref[...] + _dot(ya_ref[...], wa_ref[...]) + _dot(yb_ref[...], wb_ref[...])


def _out_proj(ya, yb, w, layer, xres, tm=1024, tn=256):
    m, k = ya.shape
    n = w.shape[-1]
    tm, tn = min(tm, m), min(tn, n)
    return pl.pallas_call(
        _out_proj_kernel,
        grid=(m // tm, n // tn),
        in_specs=[pl.BlockSpec((tm, k), lambda i, j: (i, 0)), pl.BlockSpec((tm, k), lambda i, j: (i, 0)),
                  pl.BlockSpec((None, k, tn), lambda i, j: (layer, 0, j)),
                  pl.BlockSpec((None, k, tn), lambda i, j: (layer, 1, j)),
                  pl.BlockSpec((tm, tn), lambda i, j: (i, j))],
        out_specs=pl.BlockSpec((tm, tn), lambda i, j: (i, j)),
        out_shape=jax.ShapeDtypeStruct((m, n), F32),
        compiler_params=_cparams(("parallel", "arbitrary")),
        name="out_proj",
    )(ya, yb, w, w, xres)


def _mlstm_gates_kernel(i_ref, f_ref, ib_ref, fb_ref, a_ref, b_ref, *, chunk):
    s = i_ref.shape[-1]
    li = i_ref[...] + ib_ref[...]
    lf = _log_sigmoid(f_ref[...] + fb_ref[...])
    tri = _triu01(chunk)
    for c in range(s // chunk):
        sl = slice(c * chunk, (c + 1) * chunk)
        b = _dot01(lf[:, sl], tri)
        b_ref[:, sl] = b
        a_ref[:, sl] = li[:, sl] - b


def _mlstm_gates(i_t, f_t, ig_b, fg_b, chunk):
    bsz, h, s = i_t.shape
    blk = pl.BlockSpec((None, h, s), lambda b: (b, 0, 0))
    bias = pl.BlockSpec((h, 1), lambda b: (0, 0))
    return pl.pallas_call(
        functools.partial(_mlstm_gates_kernel, chunk=chunk),
        grid=(bsz,),
        in_specs=[blk, blk, bias, bias],
        out_specs=[blk, blk],
        out_shape=[jax.ShapeDtypeStruct((bsz, h, s), F32)] * 2,
        compiler_params=_cparams(("parallel",)),
        name="mlstm_gates",
    )(i_t, f_t, ig_b.reshape(h, 1), fg_b.reshape(h, 1))


def _ssd_gates_kernel(dt_ref, dtb_ref, alog_ref, dto_ref, cum_ref, *, chunk):
    s = dt_ref.shape[-1]
    dt = _softplus(dt_ref[...] + dtb_ref[...])
    a = dt * (-jnp.exp(alog_ref[...]))
    dto_ref[...] = dt
    tri = _triu01(chunk)
    for c in range(s // chunk):
        sl = slice(c * chunk, (c + 1) * chunk)
        cum_ref[:, sl] = _dot01(a[:, sl], tri)


def _ssd_gates(dt_t, dt_b, a_log, chunk):
    bsz, h, s = dt_t.shape
    blk = pl.BlockSpec((None, h, s), lambda b: (b, 0, 0))
    par = pl.BlockSpec((h, 1), lambda b: (0, 0))
    return pl.pallas_call(
        functools.partial(_ssd_gates_kernel, chunk=chunk),
        grid=(bsz,),
        in_specs=[blk, par, par],
        out_specs=[blk, blk],
        out_shape=[jax.ShapeDtypeStruct((bsz, h, s), F32)] * 2,
        compiler_params=_cparams(("parallel",)),
        name="ssd_gates",
    )(dt_t, dt_b.reshape(h, 1), a_log.reshape(h, 1))


def _fox_gates_kernel(f_ref, fb_ref, c_ref, *, chunk):
    s = f_ref.shape[-1]
    lf = _log_sigmoid(f_ref[...] + fb_ref[...])
    tri = _triu01(chunk)
    carry = jnp.zeros((f_ref.shape[0], 1), F32)
    for c in range(s // chunk):
        sl = slice(c * chunk, (c + 1) * chunk)
        cc = _dot01(lf[:, sl], tri) + carry
        c_ref[:, sl] = cc
        carry = cc[:, chunk - 1:chunk]


def _fox_gates(f_t, fg_b, chunk=256):
    bsz, h, s = f_t.shape
    chunk = min(chunk, s)
    blk = pl.BlockSpec((None, h, s), lambda b: (b, 0, 0))
    return pl.pallas_call(
        functools.partial(_fox_gates_kernel, chunk=chunk),
        grid=(bsz,),
        in_specs=[blk, pl.BlockSpec((h, 1), lambda b: (0, 0))],
        out_specs=blk,
        out_shape=jax.ShapeDtypeStruct((bsz, h, s), F32),
        compiler_params=_cparams(("parallel",)),
        name="fox_gates",
    )(f_t, fg_b.reshape(h, 1))


def _mlstm_kernel(q_ref, k_ref, v_ref, mg_ref, arow_ref, acol_ref, bcol_ref, gain_ref, o_ref, c_scr, m_scr):
    L, dqk = q_ref.shape
    dv = v_ref.shape[-1]
    scale = dqk ** -0.5

    @pl.when(pl.program_id(2) == 0)
    def _():
        c_scr[...] = jnp.zeros_like(c_scr)
        m_scr[...] = jnp.zeros_like(m_scr)

    q = q_ref[...]
    k = k_ref[...]
    lane = lax.broadcasted_iota(jnp.int32, (L, LANES), 1)
    vext = jnp.concatenate([v_ref[...], jnp.where(lane == 0, 1.0, 0.0).astype(BF16)], axis=1)
    a_row = arow_ref[...]
    a_col = acol_ref[...]
    b_col = bcol_ref[...]
    m_prev = m_scr[0:1, 0:1]

    dmat = jnp.where(_causal_mask(L), b_col + a_row, NEG_INF)
    inter = b_col + m_prev
    m_t = jnp.maximum(inter, jnp.max(dmat, axis=1, keepdims=True))
    w = jnp.exp(dmat - m_t)
    sc = (_dot_nt(q, k) * scale * w).astype(BF16)
    g_inter = jnp.exp(inter - m_t) * scale
    numden = _dot(sc, vext) + g_inter * _dot(q, c_scr[...].astype(BF16))
    num = numden[:, :dv]
    den = numden[:, dv:dv + 1]
    h = num / jnp.maximum(jnp.abs(den), jnp.exp(-m_t))

    hn = h * lax.rsqrt(jnp.mean(h * h, axis=-1, keepdims=True) + EPS) * gain_ref[...]
    o_ref[...] = (hn * _silu(mg_ref[...].astype(F32))).astype(o_ref.dtype)

    b_last = b_col[L - 1:L, :]
    gk = b_last + a_col
    m_new = jnp.maximum(b_last + m_prev, jnp.max(gk, axis=0, keepdims=True))
    decay = jnp.exp(b_last + m_prev - m_new)
    kw = (k.astype(F32) * jnp.exp(gk - m_new)).astype(BF16)
    c_scr[...] = decay * c_scr[...] + _dot_tn(kw, vext)
    m_scr[...] = jnp.broadcast_to(m_new, m_scr.shape)


def _mlstm(u3, cols, a, b, gain, chunk):
    bsz, s, _ = u3.shape
    h, dqk, dv = MLSTM_HEADS, MLSTM_DQK, MLSTM_DV
    q0, k0, v0, g0 = (cols["q"] // dqk, cols["k"] // dqk, cols["v"] // dv, cols["m_gate"] // dv)
    a_row = a.reshape(bsz, h, 1, s)
    a_col = a.reshape(bsz, h, s, 1)
    b_col = b.reshape(bsz, h, s, 1)
    col_spec = pl.BlockSpec((None, None, chunk, 1), lambda bi, hi, ci: (bi, hi, ci, 0))
    return pl.pallas_call(
        _mlstm_kernel,
        grid=(bsz, h, s // chunk),
        in_specs=[
            pl.BlockSpec((None, chunk, dqk), lambda bi, hi, ci: (bi, ci, q0 + hi)),
            pl.BlockSpec((None, chunk, dqk), lambda bi, hi, ci: (bi, ci, k0 + hi)),
            pl.BlockSpec((None, chunk, dv), lambda bi, hi, ci: (bi, ci, v0 + hi)),
            pl.BlockSpec((None, chunk, dv), lambda bi, hi, ci: (bi, ci, g0 + hi)),
            pl.BlockSpec((None, None, 1, chunk), lambda bi, hi, ci: (bi, hi, 0, ci)),
            col_spec, col_spec,
            pl.BlockSpec((1, dv), lambda bi, hi, ci: (0, hi)),
        ],
        out_specs=pl.BlockSpec((None, chunk, dv), lambda bi, hi, ci: (bi, ci, hi)),
        out_shape=jax.ShapeDtypeStruct((bsz, s, h * dv), BF16),
        scratch_shapes=[pltpu.VMEM((dqk, dv + LANES), F32), pltpu.VMEM((SUBLANES, LANES), F32)],
        compiler_params=_cparams(("parallel", "parallel", "arbitrary")),
        name="mlstm",
    )(u3, u3, u3, u3, a_row, a_col, b_col, gain.reshape(1, h * dv))


def _ssd_kernel(xs_ref, bm_ref, cm_ref, z_ref, wx_ref, wb_ref, wc_ref, bx_ref, bb_ref, bc_ref,
                dt_ref, cum_ref, dtc_ref, cumc_ref, dskip_ref, gain_ref, o_ref,
                st_scr, xbuf, bbuf, cbuf):
    L = xs_ref.shape[0]
    P, R = SSD_HEADDIM, SSD_REP
    first = pl.program_id(2) == 0

    def conv(src_ref, buf, w_ref, b_ref):
        @pl.when(first)
        def _():
            buf[0:SUBLANES, :] = jnp.zeros((SUBLANES, buf.shape[1]), F32)

        @pl.when(jnp.logical_not(first))
        def _():
            buf[0:SUBLANES, :] = buf[L:L + SUBLANES, :]

        buf[SUBLANES:SUBLANES + L, :] = src_ref[...].astype(F32)
        acc = b_ref[...]
        for t in range(SSD_TAPS):
            off = SUBLANES - (SSD_TAPS - 1) + t
            acc = acc + w_ref[t:t + 1, :] * buf[off:off + L, :]
        return _silu(acc)

    @pl.when(first)
    def _():
        st_scr[...] = jnp.zeros_like(st_scr)

    xc = conv(xs_ref, xbuf, wx_ref, bx_ref)
    bc = conv(bm_ref, bbuf, wb_ref, bb_ref).astype(BF16)
    cc = conv(cm_ref, cbuf, wc_ref, bc_ref).astype(BF16)

    cb = _dot_nt(cc, bc)
    st = st_scr[...]
    cs = _dot(cc, st.astype(BF16))
    mask = _causal_mask(L)
    low = lax.broadcasted_iota(jnp.int32, (1, 2 * P), 1) < P

    cum_cols = cumc_ref[...]
    dt_cols = dtc_ref[...]

    def cum_col(r):
        return cum_cols[:, r:r + 1]

    def dt_col(r):
        return dt_cols[:, r:r + 1]

    def head_weights(r):
        lmat = jnp.exp(jnp.where(mask, cum_col(r) - cum_ref[r:r + 1, :], NEG_INF))
        return (cb * lmat * dt_ref[r:r + 1, :]).astype(BF16)

    y_parts, xw_parts, dec_parts = [], [], []
    for pr in range(R // 2):
        r0, r1 = 2 * pr, 2 * pr + 1
        sl = slice(pr * 2 * P, (pr + 1) * 2 * P)
        xp = xc[:, sl]
        c0, c1 = cum_col(r0), cum_col(r1)
        wts = jnp.concatenate([head_weights(r0), head_weights(r1)], axis=1)
        xbd = jnp.concatenate([jnp.where(low, xp, 0.0), jnp.where(low, 0.0, xp)], axis=0)
        y_parts.append(_dot(wts, xbd.astype(BF16)) + cs[:, sl] * jnp.where(low, jnp.exp(c0), jnp.exp(c1)))
        l0, l1 = c0[L - 1:L, :], c1[L - 1:L, :]
        xw_parts.append(xp * jnp.where(low, jnp.exp(l0 - c0) * dt_col(r0), jnp.exp(l1 - c1) * dt_col(r1)))
        dec_parts.append(jnp.where(low, jnp.exp(l0), jnp.exp(l1)))

    xw = jnp.concatenate(xw_parts, axis=1).astype(BF16)
    st_scr[...] = st * jnp.concatenate(dec_parts, axis=1) + _dot_tn(bc, xw)

    y = jnp.concatenate(y_parts, axis=1) + dskip_ref[...] * xc
    y = y * _silu(z_ref[...].astype(F32))
    yn = y * lax.rsqrt(jnp.mean(y * y, axis=-1, keepdims=True) + EPS) * gain_ref[...]
    o_ref[...] = yn.astype(o_ref.dtype)


def _ssd(u3, cols, conv_w, conv_b, dt, cum, d_skip, gain, chunk):
    bsz, s, _ = u3.shape
    g, r, gw, n = SSD_GROUPS, SSD_REP, SSD_GW, SSD_STATE
    width = g * gw
    x0, z0 = cols["xs"] // gw, cols["z"] // gw
    b0, c0 = cols["bm"] // n, cols["cm"] // n
    conv_b2 = conv_b.reshape(1, -1)
    dskip = jnp.repeat(d_skip, SSD_HEADDIM).reshape(1, width)
    dt_c = jnp.swapaxes(dt.reshape(bsz, g, r, s), 2, 3)
    cum_c = jnp.swapaxes(cum.reshape(bsz, g, r, s), 2, 3)
    rows_spec = pl.BlockSpec((None, r, chunk), lambda bi, gi, ci: (bi, gi, ci))
    cols_spec = pl.BlockSpec((None, None, chunk, r), lambda bi, gi, ci: (bi, gi, ci, 0))
    return pl.pallas_call(
        _ssd_kernel,
        grid=(bsz, g, s // chunk),
        in_specs=[
            pl.BlockSpec((None, chunk, gw), lambda bi, gi, ci: (bi, ci, x0 + gi)),
            pl.BlockSpec((None, chunk, n), lambda bi, gi, ci: (bi, ci, b0 + gi)),
            pl.BlockSpec((None, chunk, n), lambda bi, gi, ci: (bi, ci, c0 + gi)),
            pl.BlockSpec((None, chunk, gw), lambda bi, gi, ci: (bi, ci, z0 + gi)),
            pl.BlockSpec((SSD_TAPS, gw), lambda bi, gi, ci: (0, gi)),
            pl.BlockSpec((SSD_TAPS, n), lambda bi, gi, ci: (0, width // n + gi)),
            pl.BlockSpec((SSD_TAPS, n), lambda bi, gi, ci: (0, width // n + g + gi)),
            pl.BlockSpec((1, gw), lambda bi, gi, ci: (0, gi)),
            pl.BlockSpec((1, n), lambda bi, gi, ci: (0, width // n + gi)),
            pl.BlockSpec((1, n), lambda bi, gi, ci: (0, width // n + g + gi)),
            rows_spec, rows_spec, cols_spec, cols_spec,
            pl.BlockSpec((1, gw), lambda bi, gi, ci: (0, gi)),
            pl.BlockSpec((1, gw), lambda bi, gi, ci: (0, gi)),
        ],
        out_specs=pl.BlockSpec((None, chunk, gw), lambda bi, gi, ci: (bi, ci, gi)),
        out_shape=jax.ShapeDtypeStruct((bsz, s, width), BF16),
        scratch_shapes=[pltpu.VMEM((n, gw), F32),
                        pltpu.VMEM((chunk + SUBLANES, gw), F32), pltpu.VMEM((chunk + SUBLANES, n), F32),
                        pltpu.VMEM((chunk + SUBLANES, n), F32)],
        compiler_params=_cparams(("parallel", "parallel", "arbitrary")),
        name="ssd",
    )(u3, u3, u3, u3, conv_w, conv_w, conv_w, conv_b2, conv_b2, conv_b2,
      dt, cum, dt_c, cum_c, dskip, gain.reshape(1, width))


def _conformer_kernel(a_ref, g_ref, cg_ref, w_ref, b_ref, lng_ref, lnb_ref, o_ref, buf, cv):
    tm, ch = a_ref.shape
    halo = CONV_HALO
    first = pl.program_id(1) == 0

    @pl.when(first)
    def _():
        buf[0:halo, :] = jnp.zeros((halo, ch), F32)

    @pl.when(jnp.logical_not(first))
    def _():
        buf[0:halo, :] = buf[tm:tm + halo, :]

    buf[halo:halo + tm, :] = a_ref[...].astype(F32) * _sigmoid(g_ref[...].astype(F32))

    rs, cw = 128, 128
    off0 = halo - (CONV_TAPS - 1)

    def strip(idx, carry):
        c0 = pl.multiple_of(idx * cw, cw)
        for r0 in range(0, tm, rs):
            acc = jnp.broadcast_to(b_ref[:, pl.ds(c0, cw)], (rs, cw))
            for p in range(SUBLANES):
                ext = rs if p == 0 else rs + SUBLANES
                tp = None
                for t in range(CONV_TAPS):
                    if (off0 + t) % SUBLANES != p:
                        continue
                    base = r0 + off0 + t - p
                    term = w_ref[t:t + 1, pl.ds(c0, cw)] * buf[base:base + ext, pl.ds(c0, cw)]
                    tp = term if tp is None else tp + term
                acc = acc + (tp if p == 0 else tp[p:p + rs])
            cv[r0:r0 + rs, pl.ds(c0, cw)] = acc
        return carry

    lax.fori_loop(0, ch // cw, strip, 0)

    rb = 2 * SUBLANES

    def norm_rows(idx, carry):
        r0 = pl.multiple_of(idx * rb, rb)
        c = cv[pl.ds(r0, rb), :]
        mu = jnp.mean(c, axis=-1, keepdims=True)
        xc = c - mu
        y = xc * lax.rsqrt(jnp.mean(xc * xc, axis=-1, keepdims=True) + EPS)
        y = y * lng_ref[...] + lnb_ref[...]
        gate = _silu(cg_ref[pl.ds(r0, rb), :].astype(F32))
        o_ref[pl.ds(r0, rb), :] = (_silu(y) * gate).astype(o_ref.dtype)
        return carry

    lax.fori_loop(0, tm // rb, norm_rows, 0, unroll=2)


def _conformer(u3, cols, conv_w, conv_b, ln_g, ln_b, tm):
    bsz, s, _ = u3.shape
    ch = conv_w.shape[1]
    a0, g0, c0 = cols["glu_a"] // ch, cols["glu_g"] // ch, cols["c_gate"] // ch
    w_pad = jnp.concatenate([conv_w, jnp.zeros((CONV_HALO - CONV_TAPS, ch), conv_w.dtype)], axis=0)
    row = pl.BlockSpec((1, ch), lambda bi, i: (0, 0))
    return pl.pallas_call(
        _conformer_kernel,
        grid=(bsz, s // tm),
        in_specs=[
            pl.BlockSpec((None, tm, ch), lambda bi, i: (bi, i, a0)),
            pl.BlockSpec((None, tm, ch), lambda bi, i: (bi, i, g0)),
            pl.BlockSpec((None, tm, ch), lambda bi, i: (bi, i, c0)),
            pl.BlockSpec((CONV_HALO, ch), lambda bi, i: (0, 0)),
            row, row, row,
        ],
        out_specs=pl.BlockSpec((None, tm, ch), lambda bi, i: (bi, i, 0)),
        out_shape=jax.ShapeDtypeStruct((bsz, s, ch), BF16),
        scratch_shapes=[pltpu.VMEM((CONV_HALO + tm, ch), F32), pltpu.VMEM((tm, ch), F32)],
        compiler_params=_cparams(("parallel", "arbitrary")),
        name="conformer_conv",
    )(u3, u3, u3, w_pad, conv_b.reshape(1, ch), ln_g.reshape(1, ch), ln_b.reshape(1, ch))


def _fox_kernel(q_ref, k_ref, v_ref, c_ref, fg_ref, o_ref, *, tk):
    tq, d = q_ref.shape
    i = pl.program_id(2)
    q2 = (q_ref[...].astype(F32) * (d ** -0.5 * LOG2E)).astype(BF16)

    def block(j, carry, masked):
        m, l, acc = carry
        k0 = pl.multiple_of(j * tk, tk)
        s = _dot_nt(q2, k_ref[pl.ds(k0, tk), :]) - c_ref[:, pl.ds(k0, tk)] * LOG2E
        if masked:
            qpos = i * tq + lax.broadcasted_iota(jnp.int32, (tq, tk), 0)
            kpos = k0 + lax.broadcasted_iota(jnp.int32, (tq, tk), 1)
            s = jnp.where(kpos <= qpos, s, NEG_INF)
        m_new = jnp.maximum(m, jnp.max(s, axis=-1, keepdims=True))
        alpha = jnp.exp2(m - m_new)
        p = jnp.exp2(s - m_new)
        l = alpha * l + jnp.sum(p, axis=-1, keepdims=True)
        acc = alpha * acc + _dot(p.astype(BF16), v_ref[pl.ds(k0, tk), :])
        return m_new, l, acc

    carry = (jnp.full((tq, 1), NEG_INF, F32), jnp.zeros((tq, 1), F32), jnp.zeros((tq, d), F32))
    carry = block(i, carry, True)
    _, l, acc = lax.fori_loop(0, i, lambda j, c: block(j, c, False), carry)
    o_ref[...] = (acc / l * _silu(fg_ref[...].astype(F32))).astype(o_ref.dtype)


def _fox(u3, cols, c, tq, tk):
    bsz, s, _ = u3.shape
    h, d = FOX_HEADS, FOX_HEADDIM
    tq, tk = min(tq, s), min(tk, s)
    assert tq == tk, "the kernel treats exactly one key block per query tile as the masked diagonal"
    q0, k0, v0, g0 = (cols["q"] // d, cols["k"] // d, cols["v"] // d, cols["f_gate"] // d)
    c_row = c.reshape(bsz, h, 1, s)
    return pl.pallas_call(
        functools.partial(_fox_kernel, tk=tk),
        grid=(bsz, h, s // tq),
        in_specs=[
            pl.BlockSpec((None, tq, d), lambda bi, hi, i: (bi, i, q0 + hi)),
            pl.BlockSpec((None, s, d), lambda bi, hi, i: (bi, 0, k0 + hi)),
            pl.BlockSpec((None, s, d), lambda bi, hi, i: (bi, 0, v0 + hi)),
            pl.BlockSpec((None, None, 1, s), lambda bi, hi, i: (bi, hi, 0, 0)),
            pl.BlockSpec((None, tq, d), lambda bi, hi, i: (bi, i, g0 + hi)),
        ],
        out_specs=pl.BlockSpec((None, tq, d), lambda bi, hi, i: (bi, i, hi)),
        out_shape=jax.ShapeDtypeStruct((bsz, s, h * d), BF16),
        compiler_params=_cparams(("parallel", "parallel", "arbitrary")),
        name="fox_attention",
    )(u3, u3, u3, c_row, u3)


def _pad_rows(w, n):
    return jnp.concatenate([w, jnp.zeros(w.shape[:-2] + (n - w.shape[-2], w.shape[-1]), w.dtype)], axis=-2)


def _even_layer(xr, bsz, j, norm_g, w_in, ig_b, fg_b, m_norm_g, conv_w, conv_b, dt_b, a_log, d_skip, s_norm_g, w_out):
    m, d = xr.shape
    s = m // bsz
    qk, mw, sw, bc = MLSTM_HEADS * MLSTM_DQK, MLSTM_HEADS * MLSTM_DV, SSD_HEADS * SSD_HEADDIM, SSD_GROUPS * SSD_STATE
    n_a = 2 * qk + 2 * mw
    n_b = n_a + 2 * MLSTM_HEADS
    n_c = n_b + 2 * sw + 2 * bc
    w_ssd = w_in[j:j + 1, n_b:n_c]
    w_small = _pad_rows(jnp.concatenate([w_in[j:j + 1, n_a:n_b], w_in[j:j + 1, n_c:]], axis=1), LANES)
    cols_m = {"q": 0, "k": qk, "v": 2 * qk, "m_gate": 2 * qk + mw}
    cols_s = {"z": 0, "xs": sw, "bm": 2 * sw, "cm": 2 * sw + bc}

    hn = _rmsnorm(xr, norm_g, BF16)
    um = _matmul(hn, w_in, j, 0, n_a, BF16).reshape(bsz, s, n_a)
    us = _matmul(hn, w_ssd, 0, 0, n_c - n_b, BF16).reshape(bsz, s, n_c - n_b)
    ug = _matmul(hn, w_small, 0, 0, LANES, F32).reshape(bsz, s, LANES)
    hh = MLSTM_HEADS
    i_t = jnp.swapaxes(ug[:, :, :hh], 1, 2)
    f_t = jnp.swapaxes(ug[:, :, hh:2 * hh], 1, 2)
    dt_t = jnp.swapaxes(ug[:, :, 2 * hh:2 * hh + SSD_HEADS], 1, 2)

    mchunk = min(MLSTM_CHUNK, s)
    a, b = _mlstm_gates(i_t, f_t, ig_b, fg_b, mchunk)
    ya = _mlstm(um, cols_m, a, b, m_norm_g, mchunk)

    schunk = min(SSD_CHUNK, s)
    dt, cum = _ssd_gates(dt_t, dt_b, a_log, schunk)
    yb = _ssd(us, cols_s, conv_w, conv_b, dt, cum, d_skip, s_norm_g, schunk)

    return _out_proj(ya.reshape(m, mw), yb.reshape(m, sw), w_out, j, xr)


def _odd_layer(xr, bsz, j, norm_g, w_in, conv_w, conv_b, ln_g, ln_b, fg_b, w_out):
    m, d = xr.shape
    s = m // bsz
    ch = conv_w.shape[1]
    fw = FOX_HEADS * FOX_HEADDIM
    n_a = 3 * ch + 4 * fw
    w_small = _pad_rows(w_in[j:j + 1, n_a:], LANES)
    cols = {"glu_a": 0, "glu_g": ch, "c_gate": 2 * ch, "q": 3 * ch, "k": 3 * ch + fw, "v": 3 * ch + 2 * fw,
            "f_gate": 3 * ch + 3 * fw}

    hn = _rmsnorm(xr, norm_g, BF16)
    u3 = _matmul(hn, w_in, j, 0, n_a, BF16).reshape(bsz, s, n_a)
    ug = _matmul(hn, w_small, 0, 0, LANES, F32).reshape(bsz, s, LANES)
    f_t = jnp.swapaxes(ug[:, :, :FOX_HEADS], 1, 2)

    yc = _conformer(u3, cols, conv_w, conv_b, ln_g, ln_b, min(CONV_TM, s))
    c = _fox_gates(f_t, fg_b)
    yd = _fox(u3, cols, c, FOX_TQ, FOX_TK)

    return _out_proj(yc.reshape(m, ch), yd.reshape(m, fw), w_out, j, xr)


def kernel(x, e_norm_g, e_w_in, e_ig_b, e_fg_b, e_mlstm_norm_g, e_conv_w, e_conv_b, e_dt_b, e_a_log, e_d_skip, e_ssd_norm_g, e_w_out, o_norm_g, o_w_in, o_conv_w, o_conv_b, o_ln_g, o_ln_b, o_fg_b, o_w_out, final_norm_g):
    bsz, s, d = x.shape
    xr = x.reshape(bsz * s, d)
    e_w_out, o_w_out = e_w_out.astype(BF16), o_w_out.astype(BF16)
    e_w_in, o_w_in = jnp.swapaxes(e_w_in, 1, 2), jnp.swapaxes(o_w_in, 1, 2)
    for layer in range(DEPTH):
        j = layer // 2
        if layer % 2 == 0:
            xr = _even_layer(xr, bsz, j, e_norm_g[j], e_w_in, e_ig_b[j], e_fg_b[j], e_mlstm_norm_g[j], e_conv_w[j],
                             e_conv_b[j], e_dt_b[j], e_a_log[j], e_d_skip[j], e_ssd_norm_g[j], e_w_out)
        else:
            xr = _odd_layer(xr, bsz, j, o_norm_g[j], o_w_in, o_conv_w[j], o_conv_b[j], o_ln_g[j], o_ln_b[j],
                            o_fg_b[j], o_w_out)
    return _rmsnorm(xr, final_norm_g, F32).reshape(bsz, s, d)
```

```python
import functools
import math

import jax
import jax.numpy as jnp
from jax import lax
from jax.experimental import pallas as pl
from jax.experimental.pallas import tpu as pltpu

F32 = jnp.float32
BF16 = jnp.bfloat16

EPS = 1e-6
DEPTH = 4
LANES = 128
SUBLANES = 8
MLSTM_HEADS = 8
MLSTM_DQK = 256
MLSTM_DV = 512
MLSTM_CHUNK = 256
SSD_HEADS = 64
SSD_HEADDIM = 64
SSD_GROUPS = 8
SSD_REP = SSD_HEADS // SSD_GROUPS
SSD_STATE = 128
SSD_TAPS = 4
SSD_CHUNK = 128
SSD_GW = SSD_REP * SSD_HEADDIM
SSD_PACK = 2
CONV_TAPS = 31
CONV_HALO = 32
CONV_TM = 256
FOX_HEADS = 32
FOX_HEADDIM = 128
FOX_TQ = 1024
FOX_TK = 1024
VMEM_LIMIT = 56 * 1024 * 1024

NEG_INF = float("-inf")
LOG2E = math.log2(math.e)


def _cparams(sem):
    return pltpu.CompilerParams(dimension_semantics=sem, vmem_limit_bytes=VMEM_LIMIT)


def _sigmoid(x):
    return 0.5 * jnp.tanh(0.5 * x) + 0.5


def _silu(x):
    return x * _sigmoid(x)


def _softplus(x):
    return jnp.maximum(x, 0.0) + jnp.log1p(jnp.exp(-jnp.abs(x)))


def _log_sigmoid(x):
    return jnp.minimum(x, 0.0) - jnp.log1p(jnp.exp(-jnp.abs(x)))


def _dot(a, b):
    return jnp.dot(a, b, preferred_element_type=F32)


def _dot_nt(a, b):
    return lax.dot_general(a, b, (((1,), (1,)), ((), ())), preferred_element_type=F32)


def _dot_tn(a, b):
    return lax.dot_general(a, b, (((0,), (0,)), ((), ())), preferred_element_type=F32)


def _split3(x):
    hi = x.astype(BF16)
    r1 = x - hi.astype(F32)
    mid = r1.astype(BF16)
    lo = (r1 - mid.astype(F32)).astype(BF16)
    return hi, mid, lo


def _dot01(x, m01):
    hi, mid, lo = _split3(x)
    return _dot(hi, m01) + _dot(mid, m01) + _dot(lo, m01)


def _triu01(n):
    r = lax.broadcasted_iota(jnp.int32, (n, n), 0)
    c = lax.broadcasted_iota(jnp.int32, (n, n), 1)
    return jnp.where(r <= c, 1.0, 0.0).astype(BF16)


def _causal_mask(n):
    r = lax.broadcasted_iota(jnp.int32, (n, n), 0)
    c = lax.broadcasted_iota(jnp.int32, (n, n), 1)
    return c <= r


def _rmsnorm_kernel(x_ref, g_ref, o_ref):
    x = x_ref[...]
    y = x * lax.rsqrt(jnp.mean(x * x, axis=-1, keepdims=True) + EPS)
    o_ref[...] = (y * g_ref[...]).astype(o_ref.dtype)


def _rmsnorm(x, g, out_dtype, tm=256):
    m, d = x.shape
    return pl.pallas_call(
        _rmsnorm_kernel,
        grid=(m // tm,),
        in_specs=[pl.BlockSpec((tm, d), lambda i: (i, 0)), pl.BlockSpec((1, d), lambda i: (0, 0))],
        out_specs=pl.BlockSpec((tm, d), lambda i: (i, 0)),
        out_shape=jax.ShapeDtypeStruct((m, d), out_dtype),
        compiler_params=_cparams(("parallel",)),
        name="rmsnorm",
    )(x, g.reshape(1, d))


def _matmul_kernel(x_ref, wt_ref, o_ref, wb_scr):
    @pl.when(pl.program_id(1) == 0)
    def _():
        wb_scr[...] = wt_ref[...].astype(BF16)

    o_ref[...] = _dot_nt(x_ref[...], wb_scr[...]).astype(o_ref.dtype)


def _matmul(x, wt, layer, row0, n, out_dtype, tm=1024, tn=512):
    m, k = x.shape
    tm, tn = min(tm, m), min(tn, n)
    r0 = row0 // tn
    return pl.pallas_call(
        _matmul_kernel,
        grid=(n // tn, m // tm),
        in_specs=[pl.BlockSpec((tm, k), lambda j, i: (i, 0)),
                  pl.BlockSpec((None, tn, k), lambda j, i: (layer, r0 + j, 0))],
        out_specs=pl.BlockSpec((tm, tn), lambda j, i: (i, j)),
        out_shape=jax.ShapeDtypeStruct((m, n), out_dtype),
        scratch_shapes=[pltpu.VMEM((tn, k), BF16)],
        compiler_params=_cparams(("parallel", "arbitrary")),
        name="in_proj",
    )(x, wt)


def _out_proj_kernel(ya_ref, yb_ref, wa_ref, wb_ref, x_ref, o_ref):
    o_ref[...] = x_ref[...] + _dot(ya_ref[...], wa_ref[...]) + _dot(yb_ref[...], wb_ref[...])


def _out_proj(ya, yb, w, layer, xres, tm=1024, tn=256):
    m, k = ya.shape
    n = w.shape[-1]
    tm, tn = min(tm, m), min(tn, n)
    return pl.pallas_call(
        _out_proj_kernel,
        grid=(m // tm, n // tn),
        in_specs=[pl.BlockSpec((tm, k), lambda i, j: (i, 0)), pl.BlockSpec((tm, k), lambda i, j: (i, 0)),
                  pl.BlockSpec((None, k, tn), lambda i, j: (layer, 0, j)),
                  pl.BlockSpec((None, k, tn), lambda i, j: (layer, 1, j)),
                  pl.BlockSpec((tm, tn), lambda i, j: (i, j))],
        out_specs=pl.BlockSpec((tm, tn), lambda i, j: (i, j)),
        out_shape=jax.ShapeDtypeStruct((m, n), F32),
        compiler_params=_cparams(("parallel", "arbitrary")),
        name="out_proj",
    )(ya, yb, w, w, xres)


def _mlstm_gates_kernel(i_ref, f_ref, ib_ref, fb_ref, a_ref, b_ref, *, chunk):
    s = i_ref.shape[-1]
    li = i_ref[...] + ib_ref[...]
    lf = _log_sigmoid(f_ref[...] + fb_ref[...])
    tri = _triu01(chunk)
    for c in range(s // chunk):
        sl = slice(c * chunk, (c + 1) * chunk)
        b = _dot01(lf[:, sl], tri)
        b_ref[:, sl] = b
        a_ref[:, sl] = li[:, sl] - b


def _mlstm_gates(i_t, f_t, ig_b, fg_b, chunk):
    bsz, h, s = i_t.shape
    blk = pl.BlockSpec((None, h, s), lambda b: (b, 0, 0))
    bias = pl.BlockSpec((h, 1), lambda b: (0, 0))
    return pl.pallas_call(
        functools.partial(_mlstm_gates_kernel, chunk=chunk),
        grid=(bsz,),
        in_specs=[blk, blk, bias, bias],
        out_specs=[blk, blk],
        out_shape=[jax.ShapeDtypeStruct((bsz, h, s), F32)] * 2,
        compiler_params=_cparams(("parallel",)),
        name="mlstm_gates",
    )(i_t, f_t, ig_b.reshape(h, 1), fg_b.reshape(h, 1))


def _ssd_gates_kernel(dt_ref, dtb_ref, alog_ref, dto_ref, cum_ref, *, chunk):
    s = dt_ref.shape[-1]
    dt = _softplus(dt_ref[...] + dtb_ref[...])
    a = dt * (-jnp.exp(alog_ref[...]))
    dto_ref[...] = dt
    tri = _triu01(chunk)
    for c in range(s // chunk):
        sl = slice(c * chunk, (c + 1) * chunk)
        cum_ref[:, sl] = _dot01(a[:, sl], tri)


def _ssd_gates(dt_t, dt_b, a_log, chunk):
    bsz, h, s = dt_t.shape
    blk = pl.BlockSpec((None, h, s), lambda b: (b, 0, 0))
    par = pl.BlockSpec((h, 1), lambda b: (0, 0))
    return pl.pallas_call(
        functools.partial(_ssd_gates_kernel, chunk=chunk),
        grid=(bsz,),
        in_specs=[blk, par, par],
        out_specs=[blk, blk],
        out_shape=[jax.ShapeDtypeStruct((bsz, h, s), F32)] * 2,
        compiler_params=_cparams(("parallel",)),
        name="ssd_gates",
    )(dt_t, dt_b.reshape(h, 1), a_log.reshape(h, 1))


def _fox_gates_kernel(f_ref, fb_ref, c_ref, *, chunk):
    s = f_ref.shape[-1]
    lf = _log_sigmoid(f_ref[...] + fb_ref[...])
    tri = _triu01(chunk)
    carry = jnp.zeros((f_ref.shape[0], 1), F32)
    for c in range(s // chunk):
        sl = slice(c * chunk, (c + 1) * chunk)
        cc = _dot01(lf[:, sl], tri) + carry
        c_ref[:, sl] = cc
        carry = cc[:, chunk - 1:chunk]


def _fox_gates(f_t, fg_b, chunk=256):
    bsz, h, s = f_t.shape
    chunk = min(chunk, s)
    blk = pl.BlockSpec((None, h, s), lambda b: (b, 0, 0))
    return pl.pallas_call(
        functools.partial(_fox_gates_kernel, chunk=chunk),
        grid=(bsz,),
        in_specs=[blk, pl.BlockSpec((h, 1), lambda b: (0, 0))],
        out_specs=blk,
        out_shape=jax.ShapeDtypeStruct((bsz, h, s), F32),
        compiler_params=_cparams(("parallel",)),
        name="fox_gates",
    )(f_t, fg_b.reshape(h, 1))


def _mlstm_kernel(q_ref, k_ref, v_ref, mg_ref, arow_ref, acol_ref, bcol_ref, gain_ref, o_ref, c_scr, m_scr):
    L, dqk = q_ref.shape
    dv = v_ref.shape[-1]
    scale = dqk ** -0.5

    @pl.when(pl.program_id(2) == 0)
    def _():
        c_scr[...] = jnp.zeros_like(c_scr)
        m_scr[...] = jnp.zeros_like(m_scr)

    q = q_ref[...]
    k = k_ref[...]
    lane = lax.broadcasted_iota(jnp.int32, (L, LANES), 1)
    vext = jnp.concatenate([v_ref[...], jnp.where(lane == 0, 1.0, 0.0).astype(BF16)], axis=1)
    a_row = arow_ref[...]
    a_col = acol_ref[...]
    b_col = bcol_ref[...]
    m_prev = m_scr[0:1, 0:1]

    dmat = jnp.where(_causal_mask(L), b_col + a_row, NEG_INF)
    inter = b_col + m_prev
    m_t = jnp.maximum(inter, jnp.max(dmat, axis=1, keepdims=True))
    w = jnp.exp(dmat - m_t)
    sc = (_dot_nt(q, k) * scale * w).astype(BF16)
    g_inter = jnp.exp(inter - m_t) * scale
    numden = _dot(sc, vext) + g_inter * _dot(q, c_scr[...].astype(BF16))
    num = numden[:, :dv]
    den = numden[:, dv:dv + 1]
    h = num / jnp.maximum(jnp.abs(den), jnp.exp(-m_t))

    hn = h * lax.rsqrt(jnp.mean(h * h, axis=-1, keepdims=True) + EPS) * gain_ref[...]
    o_ref[...] = (hn * _silu(mg_ref[...].astype(F32))).astype(o_ref.dtype)

    b_last = b_col[L - 1:L, :]
    gk = b_last + a_col
    m_new = jnp.maximum(b_last + m_prev, jnp.max(gk, axis=0, keepdims=True))
    decay = jnp.exp(b_last + m_prev - m_new)
    kw = (k.astype(F32) * jnp.exp(gk - m_new)).astype(BF16)
    c_scr[...] = decay * c_scr[...] + _dot_tn(kw, vext)
    m_scr[...] = jnp.broadcast_to(m_new, m_scr.shape)


def _mlstm(u3, cols, a, b, gain, chunk):
    bsz, s, _ = u3.shape
    h, dqk, dv = MLSTM_HEADS, MLSTM_DQK, MLSTM_DV
    q0, k0, v0, g0 = (cols["q"] // dqk, cols["k"] // dqk, cols["v"] // dv, cols["m_gate"] // dv)
    a_row = a.reshape(bsz, h, 1, s)
    a_col = a.reshape(bsz, h, s, 1)
    b_col = b.reshape(bsz, h, s, 1)
    col_spec = pl.BlockSpec((None, None, chunk, 1), lambda bi, hi, ci: (bi, hi, ci, 0))
    return pl.pallas_call(
        _mlstm_kernel,
        grid=(bsz, h, s // chunk),
        in_specs=[
            pl.BlockSpec((None, chunk, dqk), lambda bi, hi, ci: (bi, ci, q0 + hi)),
            pl.BlockSpec((None, chunk, dqk), lambda bi, hi, ci: (bi, ci, k0 + hi)),
            pl.BlockSpec((None, chunk, dv), lambda bi, hi, ci: (bi, ci, v0 + hi)),
            pl.BlockSpec((None, chunk, dv), lambda bi, hi, ci: (bi, ci, g0 + hi)),
            pl.BlockSpec((None, None, 1, chunk), lambda bi, hi, ci: (bi, hi, 0, ci)),
            col_spec, col_spec,
            pl.BlockSpec((1, dv), lambda bi, hi, ci: (0, hi)),
        ],
        out_specs=pl.BlockSpec((None, chunk, dv), lambda bi, hi, ci: (bi, ci, hi)),
        out_shape=jax.ShapeDtypeStruct((bsz, s, h * dv), BF16),
        scratch_shapes=[pltpu.VMEM((dqk, dv + LANES), F32), pltpu.VMEM((SUBLANES, LANES), F32)],
        compiler_params=_cparams(("parallel", "parallel", "arbitrary")),
        name="mlstm",
    )(u3, u3, u3, u3, a_row, a_col, b_col, gain.reshape(1, h * dv))


def _ssd_kernel(xs_ref, bm_ref, cm_ref, z_ref, wx_ref, wb_ref, wc_ref, bx_ref, bb_ref, bc_ref,
                dt_ref, cum_ref, dtc_ref, cumc_ref, dskip_ref, gain_ref, o_ref,
                st_scr, xbuf, bbuf, cbuf):
    L = xs_ref.shape[0]
    P, R, N, GW = SSD_HEADDIM, SSD_REP, SSD_STATE, SSD_GW
    first = pl.program_id(2) == 0

    def conv(src_ref, buf, w_ref, b_ref):
        @pl.when(first)
        def _():
            buf[0:SUBLANES, :] = jnp.zeros((SUBLANES, buf.shape[1]), F32)

        @pl.when(jnp.logical_not(first))
        def _():
            buf[0:SUBLANES, :] = buf[L:L + SUBLANES, :]

        buf[SUBLANES:SUBLANES + L, :] = src_ref[...].astype(F32)
        acc = b_ref[...]
        for t in range(SSD_TAPS):
            off = SUBLANES - (SSD_TAPS - 1) + t
            acc = acc + w_ref[t:t + 1, :] * buf[off:off + L, :]
        return _silu(acc)

    @pl.when(first)
    def _():
        st_scr[...] = jnp.zeros_like(st_scr)

    xc_all = conv(xs_ref, xbuf, wx_ref, bx_ref)
    bc_all = conv(bm_ref, bbuf, wb_ref, bb_ref).astype(BF16)
    cc_all = conv(cm_ref, cbuf, wc_ref, bc_ref).astype(BF16)
    mask = _causal_mask(L)
    low = lax.broadcasted_iota(jnp.int32, (1, 2 * P), 1) < P

    for g in range(xs_ref.shape[1] // GW):
        gs = slice(g * GW, (g + 1) * GW)
        xc = xc_all[:, gs]
        bc = bc_all[:, g * N:(g + 1) * N]
        cc = cc_all[:, g * N:(g + 1) * N]
        cb = _dot_nt(cc, bc)
        st = st_scr[g]
        cs = _dot(cc, st.astype(BF16))

        cum_cols = cumc_ref[g]
        dt_cols = dtc_ref[g]

        def cum_col(r):
            return cum_cols[:, r:r + 1]

        def dt_col(r):
            return dt_cols[:, r:r + 1]

        def head_weights(r):
            row = g * R + r
            lmat = jnp.exp(jnp.where(mask, cum_col(r) - cum_ref[row:row + 1, :], NEG_INF))
            return (cb * lmat * dt_ref[row:row + 1, :]).astype(BF16)

        y_parts, xw_parts, dec_parts = [], [], []
        for pr in range(R // 2):
            r0, r1 = 2 * pr, 2 * pr + 1
            sl = slice(pr * 2 * P, (pr + 1) * 2 * P)
            xp = xc[:, sl]
            c0, c1 = cum_col(r0), cum_col(r1)
            wts = jnp.concatenate([head_weights(r0), head_weights(r1)], axis=1)
            xbd = jnp.concatenate([jnp.where(low, xp, 0.0), jnp.where(low, 0.0, xp)], axis=0)
            y_parts.append(_dot(wts, xbd.astype(BF16)) + cs[:, sl] * jnp.where(low, jnp.exp(c0), jnp.exp(c1)))
            l0, l1 = c0[L - 1:L, :], c1[L - 1:L, :]
            xw_parts.append(xp * jnp.where(low, jnp.exp(l0 - c0) * dt_col(r0), jnp.exp(l1 - c1) * dt_col(r1)))
            dec_parts.append(jnp.where(low, jnp.exp(l0), jnp.exp(l1)))

        xw = jnp.concatenate(xw_parts, axis=1).astype(BF16)
        st_scr[g] = st * jnp.concatenate(dec_parts, axis=1) + _dot_tn(bc, xw)

        y = jnp.concatenate(y_parts, axis=1) + dskip_ref[:, gs] * xc
        y = y * _silu(z_ref[:, gs].astype(F32))
        yn = y * lax.rsqrt(jnp.mean(y * y, axis=-1, keepdims=True) + EPS) * gain_ref[:, gs]
        o_ref[:, gs] = yn.astype(o_ref.dtype)


def _ssd(u3, cols, conv_w, conv_b, dt, cum, d_skip, gain, chunk):
    bsz, s, _ = u3.shape
    g, r, n, pk = SSD_GROUPS, SSD_REP, SSD_STATE, SSD_PACK
    gw, nw = pk * SSD_GW, pk * n
    width = g * SSD_GW
    x0, z0 = cols["xs"] // gw, cols["z"] // gw
    b0, c0 = cols["bm"] // nw, cols["cm"] // nw
    conv_b2 = conv_b.reshape(1, -1)
    dskip = jnp.repeat(d_skip, SSD_HEADDIM).reshape(1, width)
    dt_c = jnp.swapaxes(dt.reshape(bsz, g, r, s), 2, 3)
    cum_c = jnp.swapaxes(cum.reshape(bsz, g, r, s), 2, 3)
    rows_spec = pl.BlockSpec((None, pk * r, chunk), lambda bi, gi, ci: (bi, gi, ci))
    cols_spec = pl.BlockSpec((None, pk, chunk, r), lambda bi, gi, ci: (bi, gi, ci, 0))
    return pl.pallas_call(
        _ssd_kernel,
        grid=(bsz, g // pk, s // chunk),
        in_specs=[
            pl.BlockSpec((None, chunk, gw), lambda bi, gi, ci: (bi, ci, x0 + gi)),
            pl.BlockSpec((None, chunk, nw), lambda bi, gi, ci: (bi, ci, b0 + gi)),
            pl.BlockSpec((None, chunk, nw), lambda bi, gi, ci: (bi, ci, c0 + gi)),
            pl.BlockSpec((None, chunk, gw), lambda bi, gi, ci: (bi, ci, z0 + gi)),
            pl.BlockSpec((SSD_TAPS, gw), lambda bi, gi, ci: (0, gi)),
            pl.BlockSpec((SSD_TAPS, nw), lambda bi, gi, ci: (0, width // nw + gi)),
            pl.BlockSpec((SSD_TAPS, nw), lambda bi, gi, ci: (0, (width + g * n) // nw + gi)),
            pl.BlockSpec((1, gw), lambda bi, gi, ci: (0, gi)),
            pl.BlockSpec((1, nw), lambda bi, gi, ci: (0, width // nw + gi)),
            pl.BlockSpec((1, nw), lambda bi, gi, ci: (0, (width + g * n) // nw + gi)),
            rows_spec, rows_spec, cols_spec, cols_spec,
            pl.BlockSpec((1, gw), lambda bi, gi, ci: (0, gi)),
            pl.BlockSpec((1, gw), lambda bi, gi, ci: (0, gi)),
        ],
        out_specs=pl.BlockSpec((None, chunk, gw), lambda bi, gi, ci: (bi, ci, gi)),
        out_shape=jax.ShapeDtypeStruct((bsz, s, width), BF16),
        scratch_shapes=[pltpu.VMEM((pk, n, SSD_GW), F32),
                        pltpu.VMEM((chunk + SUBLANES, gw), F32), pltpu.VMEM((chunk + SUBLANES, nw), F32),
                        pltpu.VMEM((chunk + SUBLANES, nw), F32)],
        compiler_params=_cparams(("parallel", "parallel", "arbitrary")),
        name="ssd",
    )(u3, u3, u3, u3, conv_w, conv_w, conv_w, conv_b2, conv_b2, conv_b2,
      dt, cum, dt_c, cum_c, dskip, gain.reshape(1, width))


def _conformer_kernel(a_ref, g_ref, cg_ref, w_ref, b_ref, lng_ref, lnb_ref, o_ref, buf, cv):
    tm, ch = a_ref.shape
    halo = CONV_HALO
    first = pl.program_id(1) == 0

    @pl.when(first)
    def _():
        buf[0:halo, :] = jnp.zeros((halo, ch), F32)

    @pl.when(jnp.logical_not(first))
    def _():
        buf[0:halo, :] = buf[tm:tm + halo, :]

    buf[halo:halo + tm, :] = a_ref[...].astype(F32) * _sigmoid(g_ref[...].astype(F32))

    rs, cw = 128, 128
    off0 = halo - (CONV_TAPS - 1)

    def strip(idx, carry):
        c0 = pl.multiple_of(idx * cw, cw)
        for r0 in range(0, tm, rs):
            acc = jnp.broadcast_to(b_ref[:, pl.ds(c0, cw)], (rs, cw))
            for p in range(SUBLANES):
                ext = rs if p == 0 else rs + SUBLANES
                tp = None
                for t in range(CONV_TAPS):
                    if (off0 + t) % SUBLANES != p:
                        continue
                    base = r0 + off0 + t - p
                    term = w_ref[t:t + 1, pl.ds(c0, cw)] * buf[base:base + ext, pl.ds(c0, cw)]
                    tp = term if tp is None else tp + term
                acc = acc + (tp if p == 0 else tp[p:p + rs])
            cv[r0:r0 + rs, pl.ds(c0, cw)] = acc
        return carry

    lax.fori_loop(0, ch // cw, strip, 0)

    rb = 2 * SUBLANES

    def norm_rows(idx, carry):
        r0 = pl.multiple_of(idx * rb, rb)
        c = cv[pl.ds(r0, rb), :]
        mu = jnp.mean(c, axis=-1, keepdims=True)
        xc = c - mu
        y = xc * lax.rsqrt(jnp.mean(xc * xc, axis=-1, keepdims=True) + EPS)
        y = y * lng_ref[...] + lnb_ref[...]
        gate = _silu(cg_ref[pl.ds(r0, rb), :].astype(F32))
        o_ref[pl.ds(r0, rb), :] = (_silu(y) * gate).astype(o_ref.dtype)
        return carry

    lax.fori_loop(0, tm // rb, norm_rows, 0, unroll=2)


def _conformer(u3, cols, conv_w, conv_b, ln_g, ln_b, tm):
    bsz, s, _ = u3.shape
    ch = conv_w.shape[1]
    a0, g0, c0 = cols["glu_a"] // ch, cols["glu_g"] // ch, cols["c_gate"] // ch
    w_pad = jnp.concatenate([conv_w, jnp.zeros((CONV_HALO - CONV_TAPS, ch), conv_w.dtype)], axis=0)
    row = pl.BlockSpec((1, ch), lambda bi, i: (0, 0))
    return pl.pallas_call(
        _conformer_kernel,
        grid=(bsz, s // tm),
        in_specs=[
            pl.BlockSpec((None, tm, ch), lambda bi, i: (bi, i, a0)),
            pl.BlockSpec((None, tm, ch), lambda bi, i: (bi, i, g0)),
            pl.BlockSpec((None, tm, ch), lambda bi, i: (bi, i, c0)),
            pl.BlockSpec((CONV_HALO, ch), lambda bi, i: (0, 0)),
            row, row, row,
        ],
        out_specs=pl.BlockSpec((None, tm, ch), lambda bi, i: (bi, i, 0)),
        out_shape=jax.ShapeDtypeStruct((bsz, s, ch), BF16),
        scratch_shapes=[pltpu.VMEM((CONV_HALO + tm, ch), F32), pltpu.VMEM((tm, ch), F32)],
        compiler_params=_cparams(("parallel", "arbitrary")),
        name="conformer_conv",
    )(u3, u3, u3, w_pad, conv_b.reshape(1, ch), ln_g.reshape(1, ch), ln_b.reshape(1, ch))


def _fox_kernel(q_ref, k_ref, v_ref, c_ref, fg_ref, o_ref, *, tk):
    tq, d = q_ref.shape
    i = pl.program_id(2)
    q2 = (q_ref[...].astype(F32) * (d ** -0.5 * LOG2E)).astype(BF16)

    def block(q, k0, nk, carry, row0=None):
        m, l, acc = carry
        s = _dot_nt(q, k_ref[pl.ds(k0, nk), :]) - c_ref[:, pl.ds(k0, nk)] * LOG2E
        if row0 is not None:
            qpos = i * tq + row0 + lax.broadcasted_iota(jnp.int32, s.shape, 0)
            kpos = k0 + lax.broadcasted_iota(jnp.int32, s.shape, 1)
            s = jnp.where(kpos <= qpos, s, NEG_INF)
        m_new = jnp.maximum(m, jnp.max(s, axis=-1, keepdims=True))
        alpha = jnp.exp2(m - m_new)
        p = jnp.exp2(s - m_new)
        l = alpha * l + jnp.sum(p, axis=-1, keepdims=True)
        acc = alpha * acc + _dot(p.astype(BF16), v_ref[pl.ds(k0, nk), :])
        return m_new, l, acc

    def empty(rows):
        return jnp.full((rows, 1), NEG_INF, F32), jnp.zeros((rows, 1), F32), jnp.zeros((rows, d), F32)

    half = tq // 2
    kd = pl.multiple_of(i * tk, tk)
    top = block(q2[:half], kd, half, empty(half), row0=0)
    bot = block(q2[half:], kd, tk, empty(tq - half), row0=half)
    def body(j, carry):
        k0 = pl.multiple_of(j * tk, tk)
        return block(q2[:half], k0, tk, carry[0]), block(q2[half:], k0, tk, carry[1])

    top, bot = lax.fori_loop(0, i, body, (top, bot))
    _, l, acc = (jnp.concatenate([t, b], axis=0) for t, b in zip(top, bot))
    o_ref[...] = (acc / l * _silu(fg_ref[...].astype(F32))).astype(o_ref.dtype)


def _fox(u3, cols, c, tq, tk):
    bsz, s, _ = u3.shape
    h, d = FOX_HEADS, FOX_HEADDIM
    tq, tk = min(tq, s), min(tk, s)
    assert tq == tk, "the kernel treats exactly one key block per query tile as the masked diagonal"
    q0, k0, v0, g0 = (cols["q"] // d, cols["k"] // d, cols["v"] // d, cols["f_gate"] // d)
    c_row = c.reshape(bsz, h, 1, s)
    return pl.pallas_call(
        functools.partial(_fox_kernel, tk=tk),
        grid=(bsz, h, s // tq),
        in_specs=[
            pl.BlockSpec((None, tq, d), lambda bi, hi, i: (bi, i, q0 + hi)),
            pl.BlockSpec((None, s, d), lambda bi, hi, i: (bi, 0, k0 + hi)),
            pl.BlockSpec((None, s, d), lambda bi, hi, i: (bi, 0, v0 + hi)),
            pl.BlockSpec((None, None, 1, s), lambda bi, hi, i: (bi, hi, 0, 0)),
            pl.BlockSpec((None, tq, d), lambda bi, hi, i: (bi, i, g0 + hi)),
        ],
        out_specs=pl.BlockSpec((None, tq, d), lambda bi, hi, i: (bi, i, hi)),
        out_shape=jax.ShapeDtypeStruct((bsz, s, h * d), BF16),
        compiler_params=_cparams(("parallel", "parallel", "arbitrary")),
        name="fox_attention",
    )(u3, u3, u3, c_row, u3)


def _pad_rows(w, n):
    return jnp.concatenate([w, jnp.zeros(w.shape[:-2] + (n - w.shape[-2], w.shape[-1]), w.dtype)], axis=-2)


def _even_layer(xr, bsz, j, norm_g, w_in, ig_b, fg_b, m_norm_g, conv_w, conv_b, dt_b, a_log, d_skip, s_norm_g, w_out):
    m, d = xr.shape
    s = m // bsz
    qk, mw, sw, bc = MLSTM_HEADS * MLSTM_DQK, MLSTM_HEADS * MLSTM_DV, SSD_HEADS * SSD_HEADDIM, SSD_GROUPS * SSD_STATE
    n_a = 2 * qk + 2 * mw
    n_b = n_a + 2 * MLSTM_HEADS
    n_c = n_b + 2 * sw + 2 * bc
    w_ssd = w_in[j:j + 1, n_b:n_c]
    w_small = _pad_rows(jnp.concatenate([w_in[j:j + 1, n_a:n_b], w_in[j:j + 1, n_c:]], axis=1), LANES)
    cols_m = {"q": 0, "k": qk, "v": 2 * qk, "m_gate": 2 * qk + mw}
    cols_s = {"z": 0, "xs": sw, "bm": 2 * sw, "cm": 2 * sw + bc}

    hn = _rmsnorm(xr, norm_g, BF16)
    um = _matmul(hn, w_in, j, 0, n_a, BF16).reshape(bsz, s, n_a)
    us = _matmul(hn, w_ssd, 0, 0, n_c - n_b, BF16).reshape(bsz, s, n_c - n_b)
    ug = _matmul(hn, w_small, 0, 0, LANES, F32).reshape(bsz, s, LANES)
    hh = MLSTM_HEADS
    i_t = jnp.swapaxes(ug[:, :, :hh], 1, 2)
    f_t = jnp.swapaxes(ug[:, :, hh:2 * hh], 1, 2)
    dt_t = jnp.swapaxes(ug[:, :, 2 * hh:2 * hh + SSD_HEADS], 1, 2)

    mchunk = min(MLSTM_CHUNK, s)
    a, b = _mlstm_gates(i_t, f_t, ig_b, fg_b, mchunk)
    ya = _mlstm(um, cols_m, a, b, m_norm_g, mchunk)

    schunk = min(SSD_CHUNK, s)
    dt, cum = _ssd_gates(dt_t, dt_b, a_log, schunk)
    yb = _ssd(us, cols_s, conv_w, conv_b, dt, cum, d_skip, s_norm_g, schunk)

    return _out_proj(ya.reshape(m, mw), yb.reshape(m, sw), w_out, j, xr)


def _odd_layer(xr, bsz, j, norm_g, w_in, conv_w, conv_b, ln_g, ln_b, fg_b, w_out):
    m, d = xr.shape
    s = m // bsz
    ch = conv_w.shape[1]
    fw = FOX_HEADS * FOX_HEADDIM
    n_a = 3 * ch + 4 * fw
    w_small = _pad_rows(w_in[j:j + 1, n_a:], LANES)
    cols = {"glu_a": 0, "glu_g": ch, "c_gate": 2 * ch, "q": 3 * ch, "k": 3 * ch + fw, "v": 3 * ch + 2 * fw,
            "f_gate": 3 * ch + 3 * fw}

    hn = _rmsnorm(xr, norm_g, BF16)
    u3 = _matmul(hn, w_in, j, 0, n_a, BF16).reshape(bsz, s, n_a)
    ug = _matmul(hn, w_small, 0, 0, LANES, F32).reshape(bsz, s, LANES)
    f_t = jnp.swapaxes(ug[:, :, :FOX_HEADS], 1, 2)

    yc = _conformer(u3, cols, conv_w, conv_b, ln_g, ln_b, min(CONV_TM, s))
    c = _fox_gates(f_t, fg_b)
    yd = _fox(u3, cols, c, FOX_TQ, FOX_TK)

    return _out_proj(yc.reshape(m, ch), yd.reshape(m, fw), w_out, j, xr)


def kernel(x, e_norm_g, e_w_in, e_ig_b, e_fg_b, e_mlstm_norm_g, e_conv_w, e_conv_b, e_dt_b, e_a_log, e_d_skip, e_ssd_norm_g, e_w_out, o_norm_g, o_w_in, o_conv_w, o_conv_b, o_ln_g, o_ln_b, o_fg_b, o_w_out, final_norm_g):
    bsz, s, d = x.shape
    xr = x.reshape(bsz * s, d)
    e_w_out, o_w_out = e_w_out.astype(BF16), o_w_out.astype(BF16)
    e_w_in, o_w_in = jnp.swapaxes(e_w_in, 1, 2), jnp.swapaxes(o_w_in, 1, 2)
    for layer in range(DEPTH):
        j = layer // 2
        if layer % 2 == 0:
            xr = _even_layer(xr, bsz, j, e_norm_g[j], e_w_in, e_ig_b[j], e_fg_b[j], e_mlstm_norm_g[j], e_conv_w[j],
                             e_conv_b[j], e_dt_b[j], e_a_log[j], e_d_skip[j], e_ssd_norm_g[j], e_w_out)
        else:
            xr = _odd_layer(xr, bsz, j, o_norm_g[j], o_w_in, o_conv_w[j], o_conv_b[j], o_ln_g[j], o_ln_b[j],
                            o_fg_b[j], o_w_out)
    return _rmsnorm(xr, final_norm_g, F32).reshape(bsz, s, d)
```

```python
import functools
import math

import jax
import jax.numpy as jnp
from jax import lax
from jax.experimental import pallas as pl
from jax.experimental.pallas import tpu as pltpu

F32 = jnp.float32
BF16 = jnp.bfloat16

EPS = 1e-6
DEPTH = 4
LANES = 128
SUBLANES = 8
MLSTM_HEADS = 8
MLSTM_DQK = 256
MLSTM_DV = 512
MLSTM_CHUNK = 512
SSD_HEADS = 64
SSD_HEADDIM = 64
SSD_GROUPS = 8
SSD_REP = SSD_HEADS // SSD_GROUPS
SSD_STATE = 128
SSD_TAPS = 4
SSD_CHUNK = 128
SSD_GW = SSD_REP * SSD_HEADDIM
SSD_PACK = 2
CONV_TAPS = 31
CONV_HALO = 32
CONV_TM = 256
FOX_HEADS = 32
FOX_HEADDIM = 128
FOX_TQ = 1024
FOX_TK = 1024
VMEM_LIMIT = 56 * 1024 * 1024

NEG_INF = float("-inf")
LOG2E = math.log2(math.e)


def _cparams(sem):
    return pltpu.CompilerParams(dimension_semantics=sem, vmem_limit_bytes=VMEM_LIMIT)


def _sigmoid(x):
    return 0.5 * jnp.tanh(0.5 * x) + 0.5


def _silu(x):
    return x * _sigmoid(x)


def _softplus(x):
    return jnp.maximum(x, 0.0) + jnp.log1p(jnp.exp(-jnp.abs(x)))


def _log_sigmoid(x):
    return jnp.minimum(x, 0.0) - jnp.log1p(jnp.exp(-jnp.abs(x)))


def _dot(a, b):
    return jnp.dot(a, b, preferred_element_type=F32)


def _dot_nt(a, b):
    return lax.dot_general(a, b, (((1,), (1,)), ((), ())), preferred_element_type=F32)


def _dot_tn(a, b):
    return lax.dot_general(a, b, (((0,), (0,)), ((), ())), preferred_element_type=F32)


def _split3(x):
    hi = x.astype(BF16)
    r1 = x - hi.astype(F32)
    mid = r1.astype(BF16)
    lo = (r1 - mid.astype(F32)).astype(BF16)
    return hi, mid, lo


def _dot01(x, m01):
    hi, mid, lo = _split3(x)
    return _dot(hi, m01) + _dot(mid, m01) + _dot(lo, m01)


def _triu01(n):
    r = lax.broadcasted_iota(jnp.int32, (n, n), 0)
    c = lax.broadcasted_iota(jnp.int32, (n, n), 1)
    return jnp.where(r <= c, 1.0, 0.0).astype(BF16)


def _causal_mask(n):
    r = lax.broadcasted_iota(jnp.int32, (n, n), 0)
    c = lax.broadcasted_iota(jnp.int32, (n, n), 1)
    return c <= r


def _rmsnorm_kernel(x_ref, g_ref, o_ref):
    x = x_ref[...]
    y = x * lax.rsqrt(jnp.mean(x * x, axis=-1, keepdims=True) + EPS)
    o_ref[...] = (y * g_ref[...]).astype(o_ref.dtype)


def _rmsnorm(x, g, out_dtype, tm=256):
    m, d = x.shape
    return pl.pallas_call(
        _rmsnorm_kernel,
        grid=(m // tm,),
        in_specs=[pl.BlockSpec((tm, d), lambda i: (i, 0)), pl.BlockSpec((1, d), lambda i: (0, 0))],
        out_specs=pl.BlockSpec((tm, d), lambda i: (i, 0)),
        out_shape=jax.ShapeDtypeStruct((m, d), out_dtype),
        compiler_params=_cparams(("parallel",)),
        name="rmsnorm",
    )(x, g.reshape(1, d))


def _matmul_kernel(x_ref, wt_ref, o_ref, wb_scr):
    @pl.when(pl.program_id(1) == 0)
    def _():
        wb_scr[...] = wt_ref[...].astype(BF16)

    o_ref[...] = _dot_nt(x_ref[...], wb_scr[...]).astype(o_ref.dtype)


def _matmul(x, wt, layer, row0, n, out_dtype, tm=1024, tn=512):
    m, k = x.shape
    tm, tn = min(tm, m), min(tn, n)
    r0 = row0 // tn
    return pl.pallas_call(
        _matmul_kernel,
        grid=(n // tn, m // tm),
        in_specs=[pl.BlockSpec((tm, k), lambda j, i: (i, 0)),
                  pl.BlockSpec((None, tn, k), lambda j, i: (layer, r0 + j, 0))],
        out_specs=pl.BlockSpec((tm, tn), lambda j, i: (i, j)),
        out_shape=jax.ShapeDtypeStruct((m, n), out_dtype),
        scratch_shapes=[pltpu.VMEM((tn, k), BF16)],
        compiler_params=_cparams(("parallel", "arbitrary")),
        name="in_proj",
    )(x, wt)


def _out_proj_kernel(ya_ref, yb_ref, wa_ref, wb_ref, x_ref, o_ref):
    o_ref[...] = x_ref[...] + _dot(ya_ref[...], wa_ref[...]) + _dot(yb_ref[...], wb_ref[...])


def _out_proj(ya, yb, w, layer, xres, tm=1024, tn=256):
    m, k = ya.shape
    n = w.shape[-1]
    tm, tn = min(tm, m), min(tn, n)
    return pl.pallas_call(
        _out_proj_kernel,
        grid=(m // tm, n // tn),
        in_specs=[pl.BlockSpec((tm, k), lambda i, j: (i, 0)), pl.BlockSpec((tm, k), lambda i, j: (i, 0)),
                  pl.BlockSpec((None, k, tn), lambda i, j: (layer, 0, j)),
                  pl.BlockSpec((None, k, tn), lambda i, j: (layer, 1, j)),
                  pl.BlockSpec((tm, tn), lambda i, j: (i, j))],
        out_specs=pl.BlockSpec((tm, tn), lambda i, j: (i, j)),
        out_shape=jax.ShapeDtypeStruct((m, n), F32),
        compiler_params=_cparams(("parallel", "arbitrary")),
        name="out_proj",
    )(ya, yb, w, w, xres)


def _mlstm_gates_kernel(i_ref, f_ref, ib_ref, fb_ref, a_ref, b_ref, *, chunk):
    s = i_ref.shape[-1]
    li = i_ref[...] + ib_ref[...]
    lf = _log_sigmoid(f_ref[...] + fb_ref[...])
    tri = _triu01(chunk)
    for c in range(s // chunk):
        sl = slice(c * chunk, (c + 1) * chunk)
        b = _dot01(lf[:, sl], tri)
        b_ref[:, sl] = b
        a_ref[:, sl] = li[:, sl] - b


def _mlstm_gates(i_t, f_t, ig_b, fg_b, chunk):
    bsz, h, s = i_t.shape
    blk = pl.BlockSpec((None, h, s), lambda b: (b, 0, 0))
    bias = pl.BlockSpec((h, 1), lambda b: (0, 0))
    return pl.pallas_call(
        functools.partial(_mlstm_gates_kernel, chunk=chunk),
        grid=(bsz,),
        in_specs=[blk, blk, bias, bias],
        out_specs=[blk, blk],
        out_shape=[jax.ShapeDtypeStruct((bsz, h, s), F32)] * 2,
        compiler_params=_cparams(("parallel",)),
        name="mlstm_gates",
    )(i_t, f_t, ig_b.reshape(h, 1), fg_b.reshape(h, 1))


def _ssd_gates_kernel(dt_ref, dtb_ref, alog_ref, dto_ref, cum_ref, *, chunk):
    s = dt_ref.shape[-1]
    dt = _softplus(dt_ref[...] + dtb_ref[...])
    a = dt * (-jnp.exp(alog_ref[...]))
    dto_ref[...] = dt
    tri = _triu01(chunk)
    for c in range(s // chunk):
        sl = slice(c * chunk, (c + 1) * chunk)
        cum_ref[:, sl] = _dot01(a[:, sl], tri)


def _ssd_gates(dt_t, dt_b, a_log, chunk):
    bsz, h, s = dt_t.shape
    blk = pl.BlockSpec((None, h, s), lambda b: (b, 0, 0))
    par = pl.BlockSpec((h, 1), lambda b: (0, 0))
    return pl.pallas_call(
        functools.partial(_ssd_gates_kernel, chunk=chunk),
        grid=(bsz,),
        in_specs=[blk, par, par],
        out_specs=[blk, blk],
        out_shape=[jax.ShapeDtypeStruct((bsz, h, s), F32)] * 2,
        compiler_params=_cparams(("parallel",)),
        name="ssd_gates",
    )(dt_t, dt_b.reshape(h, 1), a_log.reshape(h, 1))


def _fox_gates_kernel(f_ref, fb_ref, c_ref, *, chunk):
    s = f_ref.shape[-1]
    lf = _log_sigmoid(f_ref[...] + fb_ref[...])
    tri = _triu01(chunk)
    carry = jnp.zeros((f_ref.shape[0], 1), F32)
    for c in range(s // chunk):
        sl = slice(c * chunk, (c + 1) * chunk)
        cc = _dot01(lf[:, sl], tri) + carry
        c_ref[:, sl] = cc
        carry = cc[:, chunk - 1:chunk]


def _fox_gates(f_t, fg_b, chunk=256):
    bsz, h, s = f_t.shape
    chunk = min(chunk, s)
    blk = pl.BlockSpec((None, h, s), lambda b: (b, 0, 0))
    return pl.pallas_call(
        functools.partial(_fox_gates_kernel, chunk=chunk),
        grid=(bsz,),
        in_specs=[blk, pl.BlockSpec((h, 1), lambda b: (0, 0))],
        out_specs=blk,
        out_shape=jax.ShapeDtypeStruct((bsz, h, s), F32),
        compiler_params=_cparams(("parallel",)),
        name="fox_gates",
    )(f_t, fg_b.reshape(h, 1))


def _mlstm_kernel(q_ref, k_ref, v_ref, mg_ref, arow_ref, acol_ref, bcol_ref, gain_ref, o_ref, c_scr, m_scr):
    L, dqk = q_ref.shape
    dv = v_ref.shape[-1]
    scale = dqk ** -0.5

    @pl.when(pl.program_id(2) == 0)
    def _():
        c_scr[...] = jnp.zeros_like(c_scr)
        m_scr[...] = jnp.zeros_like(m_scr)

    q = q_ref[...]
    k = k_ref[...]
    lane = lax.broadcasted_iota(jnp.int32, (L, LANES), 1)
    vext = jnp.concatenate([v_ref[...], jnp.where(lane == 0, 1.0, 0.0).astype(BF16)], axis=1)
    a_row = arow_ref[...]
    a_col = acol_ref[...]
    b_col = bcol_ref[...]
    m_prev = m_scr[0:1, 0:1]

    dmat = jnp.where(_causal_mask(L), b_col + a_row, NEG_INF)
    inter = b_col + m_prev
    m_t = jnp.maximum(inter, jnp.max(dmat, axis=1, keepdims=True))
    w = jnp.exp(dmat - m_t)
    sc = (_dot_nt(q, k) * scale * w).astype(BF16)
    g_inter = jnp.exp(inter - m_t) * scale
    numden = _dot(sc, vext) + g_inter * _dot(q, c_scr[...].astype(BF16))
    num = numden[:, :dv]
    den = numden[:, dv:dv + 1]
    h = num / jnp.maximum(jnp.abs(den), jnp.exp(-m_t))

    hn = h * lax.rsqrt(jnp.mean(h * h, axis=-1, keepdims=True) + EPS) * gain_ref[...]
    o_ref[...] = (hn * _silu(mg_ref[...].astype(F32))).astype(o_ref.dtype)

    b_last = b_col[L - 1:L, :]
    gk = b_last + a_col
    m_new = jnp.maximum(b_last + m_prev, jnp.max(gk, axis=0, keepdims=True))
    decay = jnp.exp(b_last + m_prev - m_new)
    kw = (k.astype(F32) * jnp.exp(gk - m_new)).astype(BF16)
    c_scr[...] = decay * c_scr[...] + _dot_tn(kw, vext)
    m_scr[...] = jnp.broadcast_to(m_new, m_scr.shape)


def _mlstm(u3, cols, a, b, gain, chunk):
    bsz, s, _ = u3.shape
    h, dqk, dv = MLSTM_HEADS, MLSTM_DQK, MLSTM_DV
    q0, k0, v0, g0 = (cols["q"] // dqk, cols["k"] // dqk, cols["v"] // dv, cols["m_gate"] // dv)
    a_row = a.reshape(bsz, h, 1, s)
    a_col = a.reshape(bsz, h, s, 1)
    b_col = b.reshape(bsz, h, s, 1)
    col_spec = pl.BlockSpec((None, None, chunk, 1), lambda bi, hi, ci: (bi, hi, ci, 0))
    return pl.pallas_call(
        _mlstm_kernel,
        grid=(bsz, h, s // chunk),
        in_specs=[
            pl.BlockSpec((None, chunk, dqk), lambda bi, hi, ci: (bi, ci, q0 + hi)),
            pl.BlockSpec((None, chunk, dqk), lambda bi, hi, ci: (bi, ci, k0 + hi)),
            pl.BlockSpec((None, chunk, dv), lambda bi, hi, ci: (bi, ci, v0 + hi)),
            pl.BlockSpec((None, chunk, dv), lambda bi, hi, ci: (bi, ci, g0 + hi)),
            pl.BlockSpec((None, None, 1, chunk), lambda bi, hi, ci: (bi, hi, 0, ci)),
            col_spec, col_spec,
            pl.BlockSpec((1, dv), lambda bi, hi, ci: (0, hi)),
        ],
        out_specs=pl.BlockSpec((None, chunk, dv), lambda bi, hi, ci: (bi, ci, hi)),
        out_shape=jax.ShapeDtypeStruct((bsz, s, h * dv), BF16),
        scratch_shapes=[pltpu.VMEM((dqk, dv + LANES), F32), pltpu.VMEM((SUBLANES, LANES), F32)],
        compiler_params=_cparams(("parallel", "parallel", "arbitrary")),
        name="mlstm",
    )(u3, u3, u3, u3, a_row, a_col, b_col, gain.reshape(1, h * dv))


def _ssd_kernel(xs_ref, bm_ref, cm_ref, z_ref, wx_ref, wb_ref, wc_ref, bx_ref, bb_ref, bc_ref,
                dt_ref, cum_ref, dtc_ref, cumc_ref, dskip_ref, gain_ref, o_ref,
                st_scr, xbuf, bbuf, cbuf):
    L = xs_ref.shape[0]
    P, R, N, GW = SSD_HEADDIM, SSD_REP, SSD_STATE, SSD_GW
    first = pl.program_id(2) == 0

    def conv(src_ref, buf, w_ref, b_ref):
        @pl.when(first)
        def _():
            buf[0:SUBLANES, :] = jnp.zeros((SUBLANES, buf.shape[1]), F32)

        @pl.when(jnp.logical_not(first))
        def _():
            buf[0:SUBLANES, :] = buf[L:L + SUBLANES, :]

        buf[SUBLANES:SUBLANES + L, :] = src_ref[...].astype(F32)
        acc = b_ref[...]
        for t in range(SSD_TAPS):
            off = SUBLANES - (SSD_TAPS - 1) + t
            acc = acc + w_ref[t:t + 1, :] * buf[off:off + L, :]
        return _silu(acc)

    @pl.when(first)
    def _():
        st_scr[...] = jnp.zeros_like(st_scr)

    xc_all = conv(xs_ref, xbuf, wx_ref, bx_ref)
    bc_all = conv(bm_ref, bbuf, wb_ref, bb_ref).astype(BF16)
    cc_all = conv(cm_ref, cbuf, wc_ref, bc_ref).astype(BF16)
    mask = _causal_mask(L)
    low = lax.broadcasted_iota(jnp.int32, (1, 2 * P), 1) < P

    for g in range(xs_ref.shape[1] // GW):
        gs = slice(g * GW, (g + 1) * GW)
        xc = xc_all[:, gs]
        bc = bc_all[:, g * N:(g + 1) * N]
        cc = cc_all[:, g * N:(g + 1) * N]
        cb = _dot_nt(cc, bc)
        st = st_scr[g]
        cs = _dot(cc, st.astype(BF16))

        cum_cols = cumc_ref[g]
        dt_cols = dtc_ref[g]

        def cum_col(r):
            return cum_cols[:, r:r + 1]

        def dt_col(r):
            return dt_cols[:, r:r + 1]

        def head_weights(r):
            row = g * R + r
            lmat = jnp.exp(jnp.where(mask, cum_col(r) - cum_ref[row:row + 1, :], NEG_INF))
            return (cb * lmat * dt_ref[row:row + 1, :]).astype(BF16)

        y_parts, xw_parts, dec_parts = [], [], []
        for pr in range(R // 2):
            r0, r1 = 2 * pr, 2 * pr + 1
            sl = slice(pr * 2 * P, (pr + 1) * 2 * P)
            xp = xc[:, sl]
            c0, c1 = cum_col(r0), cum_col(r1)
            wts = jnp.concatenate([head_weights(r0), head_weights(r1)], axis=1)
            xbd = jnp.concatenate([jnp.where(low, xp, 0.0), jnp.where(low, 0.0, xp)], axis=0)
            y_parts.append(_dot(wts, xbd.astype(BF16)) + cs[:, sl] * jnp.where(low, jnp.exp(c0), jnp.exp(c1)))
            l0, l1 = c0[L - 1:L, :], c1[L - 1:L, :]
            xw_parts.append(xp * jnp.where(low, jnp.exp(l0 - c0) * dt_col(r0), jnp.exp(l1 - c1) * dt_col(r1)))
            dec_parts.append(jnp.where(low, jnp.exp(l0), jnp.exp(l1)))

        xw = jnp.concatenate(xw_parts, axis=1).astype(BF16)
        st_scr[g] = st * jnp.concatenate(dec_parts, axis=1) + _dot_tn(bc, xw)

        y = jnp.concatenate(y_parts, axis=1) + dskip_ref[:, gs] * xc
        y = y * _silu(z_ref[:, gs].astype(F32))
        yn = y * lax.rsqrt(jnp.mean(y * y, axis=-1, keepdims=True) + EPS) * gain_ref[:, gs]
        o_ref[:, gs] = yn.astype(o_ref.dtype)


def _ssd(u3, cols, conv_w, conv_b, dt, cum, d_skip, gain, chunk):
    bsz, s, _ = u3.shape
    g, r, n, pk = SSD_GROUPS, SSD_REP, SSD_STATE, SSD_PACK
    gw, nw = pk * SSD_GW, pk * n
    width = g * SSD_GW
    x0, z0 = cols["xs"] // gw, cols["z"] // gw
    b0, c0 = cols["bm"] // nw, cols["cm"] // nw
    conv_b2 = conv_b.reshape(1, -1)
    dskip = jnp.repeat(d_skip, SSD_HEADDIM).reshape(1, width)
    dt_c = jnp.swapaxes(dt.reshape(bsz, g, r, s), 2, 3)
    cum_c = jnp.swapaxes(cum.reshape(bsz, g, r, s), 2, 3)
    rows_spec = pl.BlockSpec((None, pk * r, chunk), lambda bi, gi, ci: (bi, gi, ci))
    cols_spec = pl.BlockSpec((None, pk, chunk, r), lambda bi, gi, ci: (bi, gi, ci, 0))
    return pl.pallas_call(
        _ssd_kernel,
        grid=(bsz, g // pk, s // chunk),
        in_specs=[
            pl.BlockSpec((None, chunk, gw), lambda bi, gi, ci: (bi, ci, x0 + gi)),
            pl.BlockSpec((None, chunk, nw), lambda bi, gi, ci: (bi, ci, b0 + gi)),
            pl.BlockSpec((None, chunk, nw), lambda bi, gi, ci: (bi, ci, c0 + gi)),
            pl.BlockSpec((None, chunk, gw), lambda bi, gi, ci: (bi, ci, z0 + gi)),
            pl.BlockSpec((SSD_TAPS, gw), lambda bi, gi, ci: (0, gi)),
            pl.BlockSpec((SSD_TAPS, nw), lambda bi, gi, ci: (0, width // nw + gi)),
            pl.BlockSpec((SSD_TAPS, nw), lambda bi, gi, ci: (0, (width + g * n) // nw + gi)),
            pl.BlockSpec((1, gw), lambda bi, gi, ci: (0, gi)),
            pl.BlockSpec((1, nw), lambda bi, gi, ci: (0, width // nw + gi)),
            pl.BlockSpec((1, nw), lambda bi, gi, ci: (0, (width + g * n) // nw + gi)),
            rows_spec, rows_spec, cols_spec, cols_spec,
            pl.BlockSpec((1, gw), lambda bi, gi, ci: (0, gi)),
            pl.BlockSpec((1, gw), lambda bi, gi, ci: (0, gi)),
        ],
        out_specs=pl.BlockSpec((None, chunk, gw), lambda bi, gi, ci: (bi, ci, gi)),
        out_shape=jax.ShapeDtypeStruct((bsz, s, width), BF16),
        scratch_shapes=[pltpu.VMEM((pk, n, SSD_GW), F32),
                        pltpu.VMEM((chunk + SUBLANES, gw), F32), pltpu.VMEM((chunk + SUBLANES, nw), F32),
                        pltpu.VMEM((chunk + SUBLANES, nw), F32)],
        compiler_params=_cparams(("parallel", "parallel", "arbitrary")),
        name="ssd",
    )(u3, u3, u3, u3, conv_w, conv_w, conv_w, conv_b2, conv_b2, conv_b2,
      dt, cum, dt_c, cum_c, dskip, gain.reshape(1, width))


def _conformer_kernel(a_ref, g_ref, cg_ref, w_ref, b_ref, lng_ref, lnb_ref, o_ref, buf, cv):
    tm, ch = a_ref.shape
    halo = CONV_HALO
    first = pl.program_id(1) == 0

    @pl.when(first)
    def _():
        buf[0:halo, :] = jnp.zeros((halo, ch), F32)

    @pl.when(jnp.logical_not(first))
    def _():
        buf[0:halo, :] = buf[tm:tm + halo, :]

    buf[halo:halo + tm, :] = a_ref[...].astype(F32) * _sigmoid(g_ref[...].astype(F32))

    rs, cw = 128, 128
    off0 = halo - (CONV_TAPS - 1)

    def strip(idx, carry):
        c0 = pl.multiple_of(idx * cw, cw)
        for r0 in range(0, tm, rs):
            acc = jnp.broadcast_to(b_ref[:, pl.ds(c0, cw)], (rs, cw))
            for p in range(SUBLANES):
                ext = rs if p == 0 else rs + SUBLANES
                tp = None
                for t in range(CONV_TAPS):
                    if (off0 + t) % SUBLANES != p:
                        continue
                    base = r0 + off0 + t - p
                    term = w_ref[t:t + 1, pl.ds(c0, cw)] * buf[base:base + ext, pl.ds(c0, cw)]
                    tp = term if tp is None else tp + term
                acc = acc + (tp if p == 0 else tp[p:p + rs])
            cv[r0:r0 + rs, pl.ds(c0, cw)] = acc
        return carry

    lax.fori_loop(0, ch // cw, strip, 0)

    rb = 2 * SUBLANES

    def norm_rows(idx, carry):
        r0 = pl.multiple_of(idx * rb, rb)
        c = cv[pl.ds(r0, rb), :]
        mu = jnp.mean(c, axis=-1, keepdims=True)
        xc = c - mu
        y = xc * lax.rsqrt(jnp.mean(xc * xc, axis=-1, keepdims=True) + EPS)
        y = y * lng_ref[...] + lnb_ref[...]
        gate = _silu(cg_ref[pl.ds(r0, rb), :].astype(F32))
        o_ref[pl.ds(r0, rb), :] = (_silu(y) * gate).astype(o_ref.dtype)
        return carry

    lax.fori_loop(0, tm // rb, norm_rows, 0, unroll=2)


def _conformer(u3, cols, conv_w, conv_b, ln_g, ln_b, tm):
    bsz, s, _ = u3.shape
    ch = conv_w.shape[1]
    a0, g0, c0 = cols["glu_a"] // ch, cols["glu_g"] // ch, cols["c_gate"] // ch
    w_pad = jnp.concatenate([conv_w, jnp.zeros((CONV_HALO - CONV_TAPS, ch), conv_w.dtype)], axis=0)
    row = pl.BlockSpec((1, ch), lambda bi, i: (0, 0))
    return pl.pallas_call(
        _conformer_kernel,
        grid=(bsz, s // tm),
        in_specs=[
            pl.BlockSpec((None, tm, ch), lambda bi, i: (bi, i, a0)),
            pl.BlockSpec((None, tm, ch), lambda bi, i: (bi, i, g0)),
            pl.BlockSpec((None, tm, ch), lambda bi, i: (bi, i, c0)),
            pl.BlockSpec((CONV_HALO, ch), lambda bi, i: (0, 0)),
            row, row, row,
        ],
        out_specs=pl.BlockSpec((None, tm, ch), lambda bi, i: (bi, i, 0)),
        out_shape=jax.ShapeDtypeStruct((bsz, s, ch), BF16),
        scratch_shapes=[pltpu.VMEM((CONV_HALO + tm, ch), F32), pltpu.VMEM((tm, ch), F32)],
        compiler_params=_cparams(("parallel", "arbitrary")),
        name="conformer_conv",
    )(u3, u3, u3, w_pad, conv_b.reshape(1, ch), ln_g.reshape(1, ch), ln_b.reshape(1, ch))


def _fox_kernel(q_ref, k_ref, v_ref, c_ref, fg_ref, o_ref, *, tk):
    tq, d = q_ref.shape
    i = pl.program_id(2)
    q2 = (q_ref[...].astype(F32) * (d ** -0.5 * LOG2E)).astype(BF16)

    def block(q, k0, nk, carry, row0=None):
        m, l, acc = carry
        s = _dot_nt(q, k_ref[pl.ds(k0, nk), :]) - c_ref[:, pl.ds(k0, nk)] * LOG2E
        if row0 is not None:
            qpos = i * tq + row0 + lax.broadcasted_iota(jnp.int32, s.shape, 0)
            kpos = k0 + lax.broadcasted_iota(jnp.int32, s.shape, 1)
            s = jnp.where(kpos <= qpos, s, NEG_INF)
        m_new = jnp.maximum(m, jnp.max(s, axis=-1, keepdims=True))
        alpha = jnp.exp2(m - m_new)
        p = jnp.exp2(s - m_new)
        l = alpha * l + jnp.sum(p, axis=-1, keepdims=True)
        acc = alpha * acc + _dot(p.astype(BF16), v_ref[pl.ds(k0, nk), :])
        return m_new, l, acc

    def empty(rows):
        return jnp.full((rows, 1), NEG_INF, F32), jnp.zeros((rows, 1), F32), jnp.zeros((rows, d), F32)

    half = tq // 2
    kd = pl.multiple_of(i * tk, tk)
    top = block(q2[:half], kd, half, empty(half), row0=0)
    bot = block(q2[half:], kd, tk, empty(tq - half), row0=half)
    def body(j, carry):
        k0 = pl.multiple_of(j * tk, tk)
        return block(q2[:half], k0, tk, carry[0]), block(q2[half:], k0, tk, carry[1])

    top, bot = lax.fori_loop(0, i, body, (top, bot))
    _, l, acc = (jnp.concatenate([t, b], axis=0) for t, b in zip(top, bot))
    o_ref[...] = (acc / l * _silu(fg_ref[...].astype(F32))).astype(o_ref.dtype)


def _fox(u3, cols, c, tq, tk):
    bsz, s, _ = u3.shape
    h, d = FOX_HEADS, FOX_HEADDIM
    tq, tk = min(tq, s), min(tk, s)
    assert tq == tk, "the kernel treats exactly one key block per query tile as the masked diagonal"
    q0, k0, v0, g0 = (cols["q"] // d, cols["k"] // d, cols["v"] // d, cols["f_gate"] // d)
    c_row = c.reshape(bsz, h, 1, s)
    return pl.pallas_call(
        functools.partial(_fox_kernel, tk=tk),
        grid=(bsz, h, s // tq),
        in_specs=[
            pl.BlockSpec((None, tq, d), lambda bi, hi, i: (bi, i, q0 + hi)),
            pl.BlockSpec((None, s, d), lambda bi, hi, i: (bi, 0, k0 + hi)),
            pl.BlockSpec((None, s, d), lambda bi, hi, i: (bi, 0, v0 + hi)),
            pl.BlockSpec((None, None, 1, s), lambda bi, hi, i: (bi, hi, 0, 0)),
            pl.BlockSpec((None, tq, d), lambda bi, hi, i: (bi, i, g0 + hi)),
        ],
        out_specs=pl.BlockSpec((None, tq, d), lambda bi, hi, i: (bi, i, hi)),
        out_shape=jax.ShapeDtypeStruct((bsz, s, h * d), BF16),
        compiler_params=_cparams(("parallel", "parallel", "arbitrary")),
        name="fox_attention",
    )(u3, u3, u3, c_row, u3)


def _pad_rows(w, n):
    return jnp.concatenate([w, jnp.zeros(w.shape[:-2] + (n - w.shape[-2], w.shape[-1]), w.dtype)], axis=-2)


def _even_layer(xr, bsz, j, norm_g, w_in, ig_b, fg_b, m_norm_g, conv_w, conv_b, dt_b, a_log, d_skip, s_norm_g, w_out):
    m, d = xr.shape
    s = m // bsz
    qk, mw, sw, bc = MLSTM_HEADS * MLSTM_DQK, MLSTM_HEADS * MLSTM_DV, SSD_HEADS * SSD_HEADDIM, SSD_GROUPS * SSD_STATE
    n_a = 2 * qk + 2 * mw
    n_b = n_a + 2 * MLSTM_HEADS
    n_c = n_b + 2 * sw + 2 * bc
    w_ssd = w_in[j:j + 1, n_b:n_c]
    w_small = _pad_rows(jnp.concatenate([w_in[j:j + 1, n_a:n_b], w_in[j:j + 1, n_c:]], axis=1), LANES)
    cols_m = {"q": 0, "k": qk, "v": 2 * qk, "m_gate": 2 * qk + mw}
    cols_s = {"z": 0, "xs": sw, "bm": 2 * sw, "cm": 2 * sw + bc}

    hn = _rmsnorm(xr, norm_g, BF16)
    um = _matmul(hn, w_in, j, 0, n_a, BF16).reshape(bsz, s, n_a)
    us = _matmul(hn, w_ssd, 0, 0, n_c - n_b, BF16).reshape(bsz, s, n_c - n_b)
    ug = _matmul(hn, w_small, 0, 0, LANES, F32).reshape(bsz, s, LANES)
    hh = MLSTM_HEADS
    i_t = jnp.swapaxes(ug[:, :, :hh], 1, 2)
    f_t = jnp.swapaxes(ug[:, :, hh:2 * hh], 1, 2)
    dt_t = jnp.swapaxes(ug[:, :, 2 * hh:2 * hh + SSD_HEADS], 1, 2)

    mchunk = min(MLSTM_CHUNK, s)
    a, b = _mlstm_gates(i_t, f_t, ig_b, fg_b, mchunk)
    ya = _mlstm(um, cols_m, a, b, m_norm_g, mchunk)

    schunk = min(SSD_CHUNK, s)
    dt, cum = _ssd_gates(dt_t, dt_b, a_log, schunk)
    yb = _ssd(us, cols_s, conv_w, conv_b, dt, cum, d_skip, s_norm_g, schunk)

    return _out_proj(ya.reshape(m, mw), yb.reshape(m, sw), w_out, j, xr)


def _odd_layer(xr, bsz, j, norm_g, w_in, conv_w, conv_b, ln_g, ln_b, fg_b, w_out):
    m, d = xr.shape
    s = m // bsz
    ch = conv_w.shape[1]
    fw = FOX_HEADS * FOX_HEADDIM
    n_a = 3 * ch + 4 * fw
    w_small = _pad_rows(w_in[j:j + 1, n_a:], LANES)
    cols = {"glu_a": 0, "glu_g": ch, "c_gate": 2 * ch, "q": 3 * ch, "k": 3 * ch + fw, "v": 3 * ch + 2 * fw,
            "f_gate": 3 * ch + 3 * fw}

    hn = _rmsnorm(xr, norm_g, BF16)
    u3 = _matmul(hn, w_in, j, 0, n_a, BF16).reshape(bsz, s, n_a)
    ug = _matmul(hn, w_small, 0, 0, LANES, F32).reshape(bsz, s, LANES)
    f_t = jnp.swapaxes(ug[:, :, :FOX_HEADS], 1, 2)

    yc = _conformer(u3, cols, conv_w, conv_b, ln_g, ln_b, min(CONV_TM, s))
    c = _fox_gates(f_t, fg_b)
    yd = _fox(u3, cols, c, FOX_TQ, FOX_TK)

    return _out_proj(yc.reshape(m, ch), yd.reshape(m, fw), w_out, j, xr)


def kernel(x, e_norm_g, e_w_in, e_ig_b, e_fg_b, e_mlstm_norm_g, e_conv_w, e_conv_b, e_dt_b, e_a_log, e_d_skip, e_ssd_norm_g, e_w_out, o_norm_g, o_w_in, o_conv_w, o_conv_b, o_ln_g, o_ln_b, o_fg_b, o_w_out, final_norm_g):
    bsz, s, d = x.shape
    xr = x.reshape(bsz * s, d)
    e_w_out, o_w_out = e_w_out.astype(BF16), o_w_out.astype(BF16)
    e_w_in, o_w_in = jnp.swapaxes(e_w_in, 1, 2), jnp.swapaxes(o_w_in, 1, 2)
    for layer in range(DEPTH):
        j = layer // 2
        if layer % 2 == 0:
            xr = _even_layer(xr, bsz, j, e_norm_g[j], e_w_in, e_ig_b[j], e_fg_b[j], e_mlstm_norm_g[j], e_conv_w[j],
                             e_conv_b[j], e_dt_b[j], e_a_log[j], e_d_skip[j], e_ssd_norm_g[j], e_w_out)
        else:
            xr = _odd_layer(xr, bsz, j, o_norm_g[j], o_w_in, o_conv_w[j], o_conv_b[j], o_ln_g[j], o_ln_b[j],
                            o_fg_b[j], o_w_out)
    return _rmsnorm(xr, final_norm_g, F32).reshape(bsz, s, d)
```
